```python
import math
import jax, jax.numpy as jnp
from jax import lax
import numpy as np

D_MODEL = 1024
BATCH = 1
SEQ = 16384
DEPTH = 2
DEC_BATCH = 16
DEC_SEQ = 4096
PAST_LEN = 128

GRID_W = 64
HEAD_DIM = 64
NA_HEADS = 8
DIFF_HEADS = 4
NA_WIDTH = NA_HEADS * HEAD_DIM
DIFF_QK_WIDTH = DIFF_HEADS * 2 * HEAD_DIM
DIFF_V_DIM = 2 * HEAD_DIM
DIFF_V_WIDTH = DIFF_HEADS * DIFF_V_DIM
IN_WIDTH = 3 * NA_WIDTH + 2 * DIFF_QK_WIDTH + DIFF_V_WIDTH + 2 * D_MODEL
D_FF = 256 * ((8 * D_MODEL // 3 + 255) // 256)
WIN_H = 8
WIN_W = 16
COL_QBLOCK = 16
COL_KBLOCK = 32
ROPE_THETA = 500000.0
ROPE_DIM = HEAD_DIM // 4
Q_BLOCK = 128
NORM_EPS = 1e-6
SUBLN_EPS = 1e-5
NEG_INF = -1e30
FFN_RES = 0.5

kernel_name = "hybrid_natten_diffattn_macaron_encoder"

F32 = jnp.float32


def rms_norm(x, g, eps=NORM_EPS):
    xf = x.astype(F32)
    y = xf * lax.rsqrt(jnp.mean(xf * xf, axis=-1, keepdims=True) + eps)
    return (y * g.astype(F32)).astype(x.dtype)


def swiglu_ffn(x, wi, wo):
    a, b = jnp.split(x @ wi, 2, axis=-1)
    return (jax.nn.silu(a) * b) @ wo


def partial_rope(x, positions):
    half = ROPE_DIM // 2
    inv_freq = jnp.power(ROPE_THETA, -jnp.arange(half, dtype=F32) * 2.0 / ROPE_DIM)
    ang = positions[:, None] * inv_freq[None, :]
    cos = jnp.cos(ang)[None, :, None, None, :].astype(x.dtype)
    sin = jnp.sin(ang)[None, :, None, None, :].astype(x.dtype)
    x1 = x[..., :half]
    x2 = x[..., half:ROPE_DIM]
    rest = x[..., ROPE_DIM:]
    return jnp.concatenate([x1 * cos - x2 * sin, x2 * cos + x1 * sin, rest], axis=-1)


def neighbourhood_attention(q, k, v, rpb):
    B, L, H, DH = q.shape
    rows = L // GRID_W
    kh = min(WIN_H, rows)
    n_cb = GRID_W // COL_QBLOCK
    qcol = np.arange(GRID_W).reshape(n_cb, COL_QBLOCK)
    kc0 = np.clip(np.arange(n_cb) * COL_QBLOCK - WIN_W // 2, 0, GRID_W - COL_KBLOCK)
    kcol = kc0[:, None] + np.arange(COL_KBLOCK)
    wstart = np.clip(qcol - WIN_W // 2, 0, GRID_W - WIN_W)
    col_mask = (kcol[:, None, :] >= wstart[:, :, None]) & (kcol[:, None, :] < wstart[:, :, None] + WIN_W)
    dc = np.clip(kcol[:, None, :] - qcol[:, :, None] + WIN_W - 1, 0, 2 * WIN_W - 2)
    bias_c = jnp.where(col_mask, rpb[:, :, dc].astype(F32), NEG_INF)
    scale = DH ** -0.5
    q_rows = q.reshape(B, rows, n_cb, COL_QBLOCK, H, DH).transpose(1, 0, 2, 3, 4, 5)
    kg = k.reshape(B, rows, GRID_W, H, DH)
    vg = v.reshape(B, rows, GRID_W, H, DH)

    def row_block(args):
        r, q_row = args
        s0 = jnp.clip(r - kh // 2, 0, rows - kh)
        k_rows = lax.dynamic_slice_in_dim(kg, s0, kh, axis=1)
        v_rows = lax.dynamic_slice_in_dim(vg, s0, kh, axis=1)
        k_blk = k_rows[:, :, kcol]
        v_blk = v_rows[:, :, kcol]
        dr = s0 + jnp.arange(kh) - r + WIN_H - 1
        bias = bias_c[:, dr].transpose(0, 2, 3, 1, 4)
        s = jnp.einsum('bjqhd,bijkhd->bhjqik', q_row, k_blk).astype(F32) * scale + bias
        p = jax.nn.softmax(s.reshape(s.shape[:4] + (kh * COL_KBLOCK,)), axis=-1).reshape(s.shape)
        o = jnp.einsum('bhjqik,bijkhd->bjqhd', p.astype(v.dtype), v_blk)
        return o.reshape(B, GRID_W, H, DH)

    out = lax.map(row_block, (jnp.arange(rows), q_rows))
    return out.transpose(1, 0, 2, 3, 4).reshape(B, L, H * DH)


def differential_attention(q, k, v, lam, lam_init, subln_g):
    B, L, H, _, DH = q.shape
    nb = L // Q_BLOCK
    scale = DH ** -0.5
    q_blocks = q.reshape(B, nb, Q_BLOCK, H, 2, DH).transpose(1, 0, 2, 3, 4, 5)

    def q_block(q_blk):
        s = jnp.einsum('bqhmd,bkhmd->bhmqk', q_blk, k).astype(F32) * scale
        p = jax.nn.softmax(s, axis=-1)
        a = p[:, :, 0] - lam * p[:, :, 1]
        return jnp.einsum('bhqk,bkhe->bqhe', a.astype(v.dtype), v)

    o = lax.map(q_block, q_blocks)
    o = o.transpose(1, 0, 2, 3, 4).reshape(B, L, H, 2 * DH)
    o = rms_norm(o, subln_g, SUBLN_EPS) * (1.0 - lam_init)
    return o.reshape(B, L, H * 2 * DH)


def encoder_layer(x, layer_idx, ffn1_norm, ffn1_wi, ffn1_wo, mix_norm, w_in, qa_norm, ka_norm, rpb,
                  qb_norm, kb_norm, lam_q1, lam_k1, lam_q2, lam_k2, subln, w_a_out, w_b_out, w_o,
                  ffn2_norm, ffn2_wi, ffn2_wo):
    B, L, _ = x.shape
    x = x + FFN_RES * swiglu_ffn(rms_norm(x, ffn1_norm), ffn1_wi, ffn1_wo)
    u = rms_norm(x, mix_norm)
    proj = u @ w_in
    splits = np.cumsum([NA_WIDTH] * 3 + [DIFF_QK_WIDTH] * 2 + [DIFF_V_WIDTH, D_MODEL]).tolist()
    qa, ka, va, qb, kb, vb, ga, gb = jnp.split(proj, splits, axis=-1)
    qa = rms_norm(qa.reshape(B, L, NA_HEADS, HEAD_DIM), qa_norm)
    ka = rms_norm(ka.reshape(B, L, NA_HEADS, HEAD_DIM), ka_norm)
    va = va.reshape(B, L, NA_HEADS, HEAD_DIM)
    ya = neighbourhood_attention(qa, ka, va, rpb) @ w_a_out
    pos = jnp.arange(L, dtype=F32)
    qb = partial_rope(rms_norm(qb.reshape(B, L, DIFF_HEADS, 2, HEAD_DIM), qb_norm), pos)
    kb = partial_rope(rms_norm(kb.reshape(B, L, DIFF_HEADS, 2, HEAD_DIM), kb_norm), pos)
    vb = vb.reshape(B, L, DIFF_HEADS, DIFF_V_DIM)
    lam_init = 0.8 - 0.6 * math.exp(-0.3 * layer_idx)
    lam = (jnp.exp(jnp.sum(lam_q1.astype(F32) * lam_k1.astype(F32)))
           - jnp.exp(jnp.sum(lam_q2.astype(F32) * lam_k2.astype(F32))) + lam_init)
    yb = differential_attention(qb, kb, vb, lam, lam_init, subln) @ w_b_out
    merged = jax.nn.sigmoid(ga) * ya + jax.nn.sigmoid(gb) * yb
    x = x + merged @ w_o
    x = x + FFN_RES * swiglu_ffn(rms_norm(x, ffn2_norm), ffn2_wi, ffn2_wo)
    return x


def setup_inputs(seed: int = 0) -> dict:
    key = jax.random.key(seed)
    ks = jax.random.split(key, 24)

    def nrm(k, shape, scale):
        return jax.random.normal(k, shape, F32) * scale

    def gain(k, shape):
        return 1.0 + 0.05 * jax.random.normal(k, shape, F32)

    return {
        "x_prompt": nrm(ks[0], (BATCH, SEQ, D_MODEL), 1.0),
        "x_sample": nrm(ks[1], (DEC_BATCH, DEC_SEQ, D_MODEL), 1.0),
        "ffn1_norm": gain(ks[2], (DEPTH, D_MODEL)),
        "ffn1_wi": nrm(ks[3], (DEPTH, D_MODEL, 2 * D_FF), D_MODEL ** -0.5),
        "ffn1_wo": nrm(ks[4], (DEPTH, D_FF, D_MODEL), D_FF ** -0.5),
        "mix_norm": gain(ks[5], (DEPTH, D_MODEL)),
        "w_in": nrm(ks[6], (DEPTH, D_MODEL, IN_WIDTH), D_MODEL ** -0.5),
        "qa_norm": gain(ks[7], (DEPTH, HEAD_DIM)),
        "ka_norm": gain(ks[8], (DEPTH, HEAD_DIM)),
        "rpb": nrm(ks[9], (DEPTH, NA_HEADS, 2 * WIN_H - 1, 2 * WIN_W - 1), 0.1),
        "qb_norm": gain(ks[10], (DEPTH, HEAD_DIM)),
        "kb_norm": gain(ks[11], (DEPTH, HEAD_DIM)),
        "lam_q1": nrm(ks[12], (DEPTH, HEAD_DIM), 0.1),
        "lam_k1": nrm(ks[13], (DEPTH, HEAD_DIM), 0.1),
        "lam_q2": nrm(ks[14], (DEPTH, HEAD_DIM), 0.1),
        "lam_k2": nrm(ks[15], (DEPTH, HEAD_DIM), 0.1),
        "subln": gain(ks[16], (DEPTH, DIFF_V_DIM)),
        "w_a_out": nrm(ks[17], (DEPTH, NA_WIDTH, D_MODEL), NA_WIDTH ** -0.5),
        "w_b_out": nrm(ks[18], (DEPTH, DIFF_V_WIDTH, D_MODEL), DIFF_V_WIDTH ** -0.5),
        "w_o": nrm(ks[19], (DEPTH, D_MODEL, D_MODEL), D_MODEL ** -0.5),
        "ffn2_norm": gain(ks[20], (DEPTH, D_MODEL)),
        "ffn2_wi": nrm(ks[21], (DEPTH, D_MODEL, 2 * D_FF), D_MODEL ** -0.5),
        "ffn2_wo": nrm(ks[22], (DEPTH, D_FF, D_MODEL), D_FF ** -0.5),
    }


def reference(x_prompt, x_sample, ffn1_norm, ffn1_wi, ffn1_wo, mix_norm, w_in, qa_norm, ka_norm, rpb,
              qb_norm, kb_norm, lam_q1, lam_k1, lam_q2, lam_k2, subln, w_a_out, w_b_out, w_o,
              ffn2_norm, ffn2_wi, ffn2_wo):
    y_prompt = x_prompt
    y_sample = x_sample
    for l in range(DEPTH):
        layer_args = (ffn1_norm[l], ffn1_wi[l], ffn1_wo[l], mix_norm[l], w_in[l], qa_norm[l], ka_norm[l],
                      rpb[l], qb_norm[l], kb_norm[l], lam_q1[l], lam_k1[l], lam_q2[l], lam_k2[l], subln[l],
                      w_a_out[l], w_b_out[l], w_o[l], ffn2_norm[l], ffn2_wi[l], ffn2_wo[l])
        y_prompt = encoder_layer(y_prompt, l, *layer_args)
        y_sample = encoder_layer(y_sample, l, *layer_args)
    return (y_prompt, y_sample)
```

```python
import functools
import math

import jax
import jax.numpy as jnp
import numpy as np
from jax import lax
from jax.experimental import pallas as pl
from jax.experimental.pallas import tpu as pltpu

F32 = jnp.float32
BF16 = jnp.bfloat16

D_MODEL = 1024
GRID_W = 64
HEAD_DIM = 64
NA_HEADS = 8
DIFF_HEADS = 4
NA_WIDTH = NA_HEADS * HEAD_DIM
DIFF_V_DIM = 2 * HEAD_DIM
IN_WIDTH = 5120
D_FF = 2816
WIN_H = 8
WIN_W = 16
ROPE_THETA = 500000.0
ROPE_DIM = HEAD_DIM // 4
NORM_EPS = 1e-6
SUBLN_EPS = 1e-5
NEG_INF = -1e30
FFN_RES = 0.5

V7X_VMEM_BYTES = 64 * 1024 * 1024
V7X_LANES = 128
VMEM_LIMIT = V7X_VMEM_BYTES * 7 // 8

PROJ_BLOCK = 512
N_PROJ_BLOCKS = IN_WIDTH // PROJ_BLOCK
QB_COL = 3 * PROJ_BLOCK // V7X_LANES
KB_COL = 4 * PROJ_BLOCK // V7X_LANES
VB_COL = 5 * PROJ_BLOCK // V7X_LANES


def _params(*semantics):
    return pltpu.CompilerParams(dimension_semantics=semantics, vmem_limit_bytes=VMEM_LIMIT)


def _rms(x, g, eps):
    return x * lax.rsqrt(jnp.mean(x * x, axis=-1, keepdims=True) + eps) * g


def _ffn_kernel(x_ref, g_ref, wi_ref, wo_ref, o_ref, *, chunk):
    x = x_ref[...]
    h = _rms(x, g_ref[...], NORM_EPS).astype(BF16)
    acc = jnp.zeros(x.shape, F32)
    for c in range(D_FF // chunk):
        a = jnp.dot(h, wi_ref[:, c * chunk:(c + 1) * chunk], preferred_element_type=F32)
        b = jnp.dot(h, wi_ref[:, D_FF + c * chunk:D_FF + (c + 1) * chunk], preferred_element_type=F32)
        gate = (a * jax.nn.sigmoid(a) * b).astype(BF16)
        acc = acc + jnp.dot(gate, wo_ref[c * chunk:(c + 1) * chunk, :], preferred_element_type=F32)
    o_ref[...] = x + FFN_RES * acc


def _ffn(x, g, wi, wo, *, tm=512, chunk=1408):
    n = x.shape[0]
    return pl.pallas_call(
        functools.partial(_ffn_kernel, chunk=chunk),
        grid=(n // tm,),
        in_specs=[
            pl.BlockSpec((tm, D_MODEL), lambda i: (i, 0)),
            pl.BlockSpec((1, D_MODEL), lambda i: (0, 0)),
            pl.BlockSpec((D_MODEL, 2 * D_FF), lambda i: (0, 0)),
            pl.BlockSpec((D_FF, D_MODEL), lambda i: (0, 0)),
        ],
        out_specs=pl.BlockSpec((tm, D_MODEL), lambda i: (i, 0)),
        out_shape=jax.ShapeDtypeStruct((n, D_MODEL), F32),
        compiler_params=_params("parallel"),
        name="ffn",
    )(x, g, wi, wo)


def _proj_kernel(x_ref, g_ref, w_ref, hg_ref, bd_ref, cos_ref, sa_ref, sb_ref, o_ref, h_scr):
    j = pl.program_id(1)

    @pl.when(j == 0)
    def _():
        h_scr[...] = _rms(x_ref[...], g_ref[...], NORM_EPS).astype(BF16)

    p = jnp.dot(h_scr[...], w_ref[...], preferred_element_type=F32)

    def head_norm():
        ms = jnp.dot((p * p).astype(BF16), bd_ref[...], preferred_element_type=F32)
        return p * lax.rsqrt(ms + NORM_EPS) * hg_ref[...]

    @pl.when(j < 2)
    def _():
        o_ref[...] = head_norm().astype(o_ref.dtype)

    @pl.when((j == 3) | (j == 4))
    def _():
        y = head_norm()
        reps = PROJ_BLOCK // V7X_LANES
        cos = jnp.concatenate([cos_ref[...]] * reps, axis=1)
        sa = jnp.concatenate([sa_ref[...]] * reps, axis=1)
        sb = jnp.concatenate([sb_ref[...]] * reps, axis=1)
        half = ROPE_DIM // 2
        up = pltpu.roll(y, PROJ_BLOCK - half, 1)
        dn = pltpu.roll(y, half, 1)
        o_ref[...] = (y * cos + up * sa + dn * sb).astype(o_ref.dtype)

    @pl.when((j == 2) | (j == 5))
    def _():
        o_ref[...] = p.astype(o_ref.dtype)

    @pl.when(j >= 6)
    def _():
        o_ref[...] = jax.nn.sigmoid(p).astype(o_ref.dtype)


def _in_proj(x, g, w, head_gains, blockdiag, cos, sa, sb, seq_len, *, tm=512):
    n = x.shape[0]
    tiles_per_seq = seq_len // tm
    return pl.pallas_call(
        _proj_kernel,
        grid=(n // tm, N_PROJ_BLOCKS),
        in_specs=[
            pl.BlockSpec((tm, D_MODEL), lambda i, j: (i, 0)),
            pl.BlockSpec((1, D_MODEL), lambda i, j: (0, 0)),
            pl.BlockSpec((D_MODEL, PROJ_BLOCK), lambda i, j: (0, j)),
            pl.BlockSpec((None, 1, PROJ_BLOCK), lambda i, j: (j, 0, 0)),
            pl.BlockSpec((PROJ_BLOCK, PROJ_BLOCK), lambda i, j: (0, 0)),
            pl.BlockSpec((tm, V7X_LANES), lambda i, j: (i % tiles_per_seq, 0)),
            pl.BlockSpec((tm, V7X_LANES), lambda i, j: (i % tiles_per_seq, 0)),
            pl.BlockSpec((tm, V7X_LANES), lambda i, j: (i % tiles_per_seq, 0)),
        ],
        out_specs=pl.BlockSpec((tm, PROJ_BLOCK), lambda i, j: (i, j)),
        out_shape=jax.ShapeDtypeStruct((n, IN_WIDTH), BF16),
        scratch_shapes=[pltpu.VMEM((tm, D_MODEL), BF16)],
        compiler_params=_params("parallel", "arbitrary"),
        name="in_proj",
    )(x, g, w, head_gains, blockdiag, cos, sa, sb)


def _natten_kernel(q_ref, k_ref, v_ref, bias_ref, o_ref):
    rows_q = q_ref.shape[0]
    lane = lax.broadcasted_iota(jnp.int32, (rows_q, V7X_LANES), 1)
    lo = lane < HEAD_DIM
    for pair in range(NA_HEADS // 2):
        cols = slice(pair * V7X_LANES, (pair + 1) * V7X_LANES)
        qp = q_ref[:, cols]
        zero = jnp.zeros_like(qp)
        lhs = jnp.concatenate([jnp.where(lo, qp, zero), jnp.where(lo, zero, qp)], axis=0)
        s = lax.dot_general(lhs, k_ref[:, cols], (((1,), (1,)), ((), ())), preferred_element_type=F32)
        s = s + bias_ref[pair * 2 * rows_q:(pair + 1) * 2 * rows_q, :]
        m = jnp.max(s, axis=-1, keepdims=True)
        e = jnp.exp(s - m)
        l = jnp.sum(e, axis=-1, keepdims=True)
        o = jnp.dot(e.astype(BF16), v_ref[:, cols], preferred_element_type=F32)
        o = o * (1.0 / l)
        o_ref[:, cols] = jnp.where(lo, o[:rows_q], o[rows_q:]).astype(o_ref.dtype)


def _natten(proj, bias, batch, seq_len):
    rows = seq_len // GRID_W
    n = proj.shape[0]
    n_keys = WIN_H * GRID_W

    def start_row(r):
        return jnp.clip(r - WIN_H // 2, 0, rows - WIN_H)

    return pl.pallas_call(
        _natten_kernel,
        grid=(batch, rows),
        in_specs=[
            pl.BlockSpec((GRID_W, NA_WIDTH), lambda b, r: (b * rows + r, 0)),
            pl.BlockSpec((pl.Element(n_keys), pl.Element(NA_WIDTH)),
                         lambda b, r: ((b * rows + start_row(r)) * GRID_W, NA_WIDTH)),
            pl.BlockSpec((pl.Element(n_keys), pl.Element(NA_WIDTH)),
                         lambda b, r: ((b * rows + start_row(r)) * GRID_W, 2 * NA_WIDTH)),
            pl.BlockSpec((None, NA_HEADS * GRID_W, n_keys),
                         lambda b, r: (start_row(r) - r + WIN_H - 1, 0, 0)),
        ],
        out_specs=pl.BlockSpec((GRID_W, NA_WIDTH), lambda b, r: (b * rows + r, 0)),
        out_shape=jax.ShapeDtypeStruct((n, NA_WIDTH), BF16),
        compiler_params=_params("parallel", "arbitrary"),
        name="natten",
    )(proj, proj, proj, bias)


def _natten_bias(rpb):
    qc = np.arange(GRID_W)[:, None]
    kc = np.arange(GRID_W)[None, :]
    wstart = np.clip(qc - WIN_W // 2, 0, GRID_W - WIN_W)
    in_win = (kc >= wstart) & (kc < wstart + WIN_W)
    dc = np.clip(kc - qc + WIN_W - 1, 0, 2 * WIN_W - 2)
    dr = np.arange(WIN_H)[:, None] + np.arange(WIN_H)[None, :]
    b = rpb.astype(F32)[:, dr][:, :, :, dc]
    b = jnp.where(in_win[None, None, None], b, NEG_INF)
    b = b.transpose(1, 0, 3, 2, 4)
    return b.reshape(WIN_H, NA_HEADS * GRID_W, WIN_H * GRID_W)


def _diff_kernel(sc_ref, q_ref, k_ref, v_ref, g_ref, o_ref, m_scr, l_scr, acc_scr, *, tk):
    tq = q_ref.shape[0]
    q = q_ref[...]
    lane = lax.broadcasted_iota(jnp.int32, q.shape, 1)
    zero = jnp.zeros_like(q)
    qs = jnp.concatenate([jnp.where(lane < HEAD_DIM, q, zero), jnp.where(lane < HEAD_DIM, zero, q)], axis=0)
    m_scr[...] = jnp.full(m_scr.shape, NEG_INF, F32)
    l_scr[...] = jnp.zeros(l_scr.shape, F32)
    acc_scr[...] = jnp.zeros(acc_scr.shape, F32)

    def body(c, carry):
        start = pl.multiple_of(c * tk, tk)
        k = k_ref[pl.ds(start, tk), :]
        v = v_ref[pl.ds(start, tk), :]
        s = lax.dot_general(qs, k, (((1,), (1,)), ((), ())), preferred_element_type=F32)
        m_prev = m_scr[...]
        m_new = jnp.maximum(m_prev, jnp.max(s, axis=-1, keepdims=True))
        alpha = jnp.exp(m_prev - m_new)
        p = jnp.exp(s - m_new)
        l_scr[...] = alpha * l_scr[...] + jnp.sum(p, axis=-1, keepdims=True)
        acc_scr[...] = alpha * acc_scr[...] + jnp.dot(p.astype(BF16), v, preferred_element_type=F32)
        m_scr[...] = m_new
        return carry

    lax.fori_loop(0, k_ref.shape[0] // tk, body, 0)

    o = acc_scr[...] * (1.0 / l_scr[...])
    lam = sc_ref[0]
    d = o[:tq] - lam * o[tq:]
    o_ref[...] = (_rms(d, g_ref[...], SUBLN_EPS) * sc_ref[1]).astype(o_ref.dtype)


def _diffattn(proj, scalars, subln_g, batch, seq_len, *, tq=256, tk=512):
    n = proj.shape[0]
    tq = min(tq, seq_len)
    tk = min(tk, seq_len)
    q_tiles = seq_len // tq
    return pl.pallas_call(
        functools.partial(_diff_kernel, tk=tk),
        grid=(batch, DIFF_HEADS, q_tiles),
        in_specs=[
            pl.BlockSpec(memory_space=pltpu.SMEM),
            pl.BlockSpec((tq, DIFF_V_DIM), lambda b, h, i: (b * q_tiles + i, QB_COL + h)),
            pl.BlockSpec((seq_len, DIFF_V_DIM), lambda b, h, i: (b, KB_COL + h)),
            pl.BlockSpec((seq_len, DIFF_V_DIM), lambda b, h, i: (b, VB_COL + h)),
            pl.BlockSpec((1, DIFF_V_DIM), lambda b, h, i: (0, 0)),
        ],
        out_specs=pl.BlockSpec((tq, DIFF_V_DIM), lambda b, h, i: (b * q_tiles + i, h)),
        out_shape=jax.ShapeDtypeStruct((n, DIFF_HEADS * DIFF_V_DIM), BF16),
        scratch_shapes=[
            pltpu.VMEM((2 * tq, 1), F32),
            pltpu.VMEM((2 * tq, 1), F32),
            pltpu.VMEM((2 * tq, DIFF_V_DIM), F32),
        ],
        compiler_params=_params("parallel", "parallel", "arbitrary"),
        name="diffattn",
    )(scalars, proj, proj, proj, subln_g)


def _merge_kernel(x_ref, ya_ref, yb_ref, ga_ref, gb_ref, wa_ref, wb_ref, wo_ref, o_ref):
    a = jnp.dot(ya_ref[...], wa_ref[...], preferred_element_type=F32)
    b = jnp.dot(yb_ref[...], wb_ref[...], preferred_element_type=F32)
    merged = ga_ref[...].astype(F32) * a + gb_ref[...].astype(F32) * b
    o_ref[...] = x_ref[...] + jnp.dot(merged.astype(BF16), wo_ref[...], preferred_element_type=F32)


def _merge(x, ya, yb, proj, wa, wb, wo, *, tm=512):
    n = x.shape[0]
    ga_col = 3 * PROJ_BLOCK * 2 // D_MODEL
    return pl.pallas_call(
        _merge_kernel,
        grid=(n // tm,),
        in_specs=[
            pl.BlockSpec((tm, D_MODEL), lambda i: (i, 0)),
            pl.BlockSpec((tm, NA_WIDTH), lambda i: (i, 0)),
            pl.BlockSpec((tm, NA_WIDTH), lambda i: (i, 0)),
            pl.BlockSpec((tm, D_MODEL), lambda i: (i, ga_col)),
            pl.BlockSpec((tm, D_MODEL), lambda i: (i, ga_col + 1)),
            pl.BlockSpec((NA_WIDTH, D_MODEL), lambda i: (0, 0)),
            pl.BlockSpec((NA_WIDTH, D_MODEL), lambda i: (0, 0)),
            pl.BlockSpec((D_MODEL, D_MODEL), lambda i: (0, 0)),
        ],
        out_specs=pl.BlockSpec((tm, D_MODEL), lambda i: (i, 0)),
        out_shape=jax.ShapeDtypeStruct((n, D_MODEL), F32),
        compiler_params=_params("parallel"),
        name="merge",
    )(x, ya, yb, proj, proj, wa, wb, wo)


def _rope_tables(seq_len):
    half = ROPE_DIM // 2
    inv_freq = jnp.power(ROPE_THETA, -jnp.arange(half, dtype=F32) * 2.0 / ROPE_DIM)
    ang = jnp.arange(seq_len, dtype=F32)[:, None] * inv_freq[None, :]
    cos, sin = jnp.cos(ang), jnp.sin(ang)
    ones = jnp.ones((seq_len, HEAD_DIM - ROPE_DIM), F32)
    zeros = jnp.zeros((seq_len, HEAD_DIM - ROPE_DIM), F32)
    zh = jnp.zeros_like(sin)
    c = jnp.concatenate([cos, cos, ones], axis=1)
    sa = jnp.concatenate([-sin, zh, zeros], axis=1)
    sb = jnp.concatenate([zh, sin, zeros], axis=1)
    reps = V7X_LANES // HEAD_DIM
    return tuple(jnp.tile(t, (1, reps)) for t in (c, sa, sb))


def _layer_consts(l, w):
    scale = HEAD_DIM ** -0.5
    heads = PROJ_BLOCK // HEAD_DIM
    ones = jnp.ones((PROJ_BLOCK,), F32)
    head_gains = jnp.stack([
        jnp.tile(w["qa_norm"][l] * scale, heads), jnp.tile(w["ka_norm"][l], heads), ones,
        jnp.tile(w["qb_norm"][l] * scale, heads), jnp.tile(w["kb_norm"][l], heads), ones,
        ones, ones, ones, ones])[:, None, :]
    blk = np.arange(PROJ_BLOCK) // HEAD_DIM
    blockdiag = jnp.asarray((blk[:, None] == blk[None, :]).astype(np.float32) / HEAD_DIM, BF16)
    lam_init = 0.8 - 0.6 * math.exp(-0.3 * l)
    lam = (jnp.exp(jnp.sum(w["lam_q1"][l] * w["lam_k1"][l]))
           - jnp.exp(jnp.sum(w["lam_q2"][l] * w["lam_k2"][l])) + lam_init)
    return dict(
        ffn1_g=w["ffn1_norm"][l][None], ffn1_wi=w["ffn1_wi"][l].astype(BF16), ffn1_wo=w["ffn1_wo"][l].astype(BF16),
        mix_g=w["mix_norm"][l][None], w_in=w["w_in"][l].astype(BF16),
        head_gains=head_gains, blockdiag=blockdiag, bias=_natten_bias(w["rpb"][l]),
        scalars=jnp.stack([lam, jnp.asarray(1.0 - lam_init, F32)]).astype(F32),
        subln=w["subln"][l][None],
        w_a=w["w_a_out"][l].astype(BF16), w_b=w["w_b_out"][l].astype(BF16), w_o=w["w_o"][l].astype(BF16),
        ffn2_g=w["ffn2_norm"][l][None], ffn2_wi=w["ffn2_wi"][l].astype(BF16), ffn2_wo=w["ffn2_wo"][l].astype(BF16),
    )


def _encoder_layer(x, c, rope, batch, seq_len):
    x = _ffn(x, c["ffn1_g"], c["ffn1_wi"], c["ffn1_wo"])
    proj = _in_proj(x, c["mix_g"], c["w_in"], c["head_gains"], c["blockdiag"], *rope, seq_len)
    ya = _natten(proj, c["bias"], batch, seq_len)
    yb = _diffattn(proj, c["scalars"], c["subln"], batch, seq_len)
    x = _merge(x, ya, yb, proj, c["w_a"], c["w_b"], c["w_o"])
    return _ffn(x, c["ffn2_g"], c["ffn2_wi"], c["ffn2_wo"])


def _encoder(groups, weights, depth):
    consts = [_layer_consts(l, weights) for l in range(depth)]
    outs = []
    for x in groups:
        batch, seq_len, _ = x.shape
        rope = _rope_tables(seq_len)
        y = x.reshape(batch * seq_len, D_MODEL)
        for c in consts:
            y = _encoder_layer(y, c, rope, batch, seq_len)
        outs.append(y.reshape(x.shape))
    return tuple(outs)


def kernel(x_prompt, x_sample, ffn1_norm, ffn1_wi, ffn1_wo, mix_norm, w_in, qa_norm, ka_norm, rpb, qb_norm, kb_norm, lam_q1, lam_k1, lam_q2, lam_k2, subln, w_a_out, w_b_out, w_o, ffn2_norm, ffn2_wi, ffn2_wo):
    weights = dict(ffn1_norm=ffn1_norm, ffn1_wi=ffn1_wi, ffn1_wo=ffn1_wo, mix_norm=mix_norm, w_in=w_in,
                   qa_norm=qa_norm, ka_norm=ka_norm, rpb=rpb, qb_norm=qb_norm, kb_norm=kb_norm,
                   lam_q1=lam_q1, lam_k1=lam_k1, lam_q2=lam_q2, lam_k2=lam_k2, subln=subln,
                   w_a_out=w_a_out, w_b_out=w_b_out, w_o=w_o,
                   ffn2_norm=ffn2_norm, ffn2_wi=ffn2_wi, ffn2_wo=ffn2_wo)
    return _encoder((x_prompt, x_sample), weights, ffn1_norm.shape[0])
```

```python
import functools
import math

import jax
import jax.numpy as jnp
import numpy as np
from jax import lax
from jax.experimental import pallas as pl
from jax.experimental.pallas import tpu as pltpu

F32 = jnp.float32
BF16 = jnp.bfloat16

D_MODEL = 1024
GRID_W = 64
HEAD_DIM = 64
NA_HEADS = 8
DIFF_HEADS = 4
NA_WIDTH = NA_HEADS * HEAD_DIM
DIFF_V_DIM = 2 * HEAD_DIM
IN_WIDTH = 5120
D_FF = 2816
WIN_H = 8
WIN_W = 16
ROPE_THETA = 500000.0
ROPE_DIM = HEAD_DIM // 4
NORM_EPS = 1e-6
SUBLN_EPS = 1e-5
NEG_INF = -1e30
FFN_RES = 0.5

V7X_VMEM_BYTES = 64 * 1024 * 1024
V7X_LANES = 128
VMEM_LIMIT = V7X_VMEM_BYTES * 7 // 8

PROJ_BLOCK = 512
W_QA, W_KA, W_VA, W_QB, W_KB, W_VB, W_GA, W_GB = 0, 512, 1024, 1536, 2048, 2560, 3072, 4096
GA_OFF, GB_OFF, QA_OFF, KA_OFF, VA_OFF, QB_OFF, KB_OFF = 0, 1024, 2048, 2560, 3072, 3584, 4096
PROJ_OUT = 4608
DIFF_TK = 256
LOG2_E = math.log2(math.e)


def _params(*semantics):
    return pltpu.CompilerParams(dimension_semantics=semantics, vmem_limit_bytes=VMEM_LIMIT)


def _rms(x, g, eps):
    return x * lax.rsqrt(jnp.mean(x * x, axis=-1, keepdims=True) + eps) * g


def _ffn_kernel(x_ref, g_ref, wi_ref, wo_ref, o_ref, *, chunk):
    x = x_ref[...]
    h = _rms(x, g_ref[...], NORM_EPS).astype(BF16)
    acc = jnp.zeros(x.shape, F32)
    for c in range(D_FF // chunk):
        a = jnp.dot(h, wi_ref[:, c * chunk:(c + 1) * chunk], preferred_element_type=F32)
        b = jnp.dot(h, wi_ref[:, D_FF + c * chunk:D_FF + (c + 1) * chunk], preferred_element_type=F32)
        gate = (a * jax.nn.sigmoid(a) * b).astype(BF16)
        acc = acc + jnp.dot(gate, wo_ref[c * chunk:(c + 1) * chunk, :], preferred_element_type=F32)
    o_ref[...] = x + FFN_RES * acc


def _ffn(x, g, wi, wo, *, tm=512, chunk=1408):
    n = x.shape[0]
    return pl.pallas_call(
        functools.partial(_ffn_kernel, chunk=chunk),
        grid=(n // tm,),
        in_specs=[
            pl.BlockSpec((tm, D_MODEL), lambda i: (i, 0)),
            pl.BlockSpec((1, D_MODEL), lambda i: (0, 0)),
            pl.BlockSpec((D_MODEL, 2 * D_FF), lambda i: (0, 0)),
            pl.BlockSpec((D_FF, D_MODEL), lambda i: (0, 0)),
        ],
        out_specs=pl.BlockSpec((tm, D_MODEL), lambda i: (i, 0)),
        out_shape=jax.ShapeDtypeStruct((n, D_MODEL), F32),
        compiler_params=_params("parallel"),
        name="ffn",
    )(x, g, wi, wo)


def _proj_kernel(x_ref, g_ref, w_ref, hg_ref, bd_ref, cos_ref, sa_ref, sb_ref, o_ref, vt_ref):
    h = _rms(x_ref[...], g_ref[...], NORM_EPS).astype(BF16)
    tk = vt_ref.shape[-1]

    def proj(w_off):
        return jnp.dot(h, w_ref[:, w_off:w_off + PROJ_BLOCK], preferred_element_type=F32)

    def head_norm(p, gain_row):
        ms = jnp.dot((p * p).astype(BF16), bd_ref[...], preferred_element_type=F32)
        return p * lax.rsqrt(ms + NORM_EPS) * hg_ref[gain_row:gain_row + 1, :]

    reps = PROJ_BLOCK // V7X_LANES
    cos = jnp.concatenate([cos_ref[...]] * reps, axis=1)
    sa = jnp.concatenate([sa_ref[...]] * reps, axis=1)
    sb = jnp.concatenate([sb_ref[...]] * reps, axis=1)

    def rope(y):
        half = ROPE_DIM // 2
        up = pltpu.roll(y, PROJ_BLOCK - half, 1)
        dn = pltpu.roll(y, half, 1)
        return y * cos + up * sa + dn * sb

    def put(off, val):
        o_ref[:, off:off + PROJ_BLOCK] = val.astype(o_ref.dtype)

    put(QA_OFF, head_norm(proj(W_QA), 0))
    put(KA_OFF, head_norm(proj(W_KA), 1))
    put(VA_OFF, proj(W_VA))
    put(QB_OFF, rope(head_norm(proj(W_QB), 2)))
    put(KB_OFF, rope(head_norm(proj(W_KB), 3)))
    vb = proj(W_VB)
    for head in range(DIFF_HEADS):
        for c in range(vt_ref.shape[1]):
            blk = vb[c * tk:(c + 1) * tk, head * DIFF_V_DIM:(head + 1) * DIFF_V_DIM]
            vt_ref[head, c] = blk.T.astype(vt_ref.dtype)
    for half in range(D_MODEL // PROJ_BLOCK):
        put(GA_OFF + half * PROJ_BLOCK, jax.nn.sigmoid(proj(W_GA + half * PROJ_BLOCK)))
        put(GB_OFF + half * PROJ_BLOCK, jax.nn.sigmoid(proj(W_GB + half * PROJ_BLOCK)))


def _in_proj(x, g, w, head_gains, blockdiag, cos, sa, sb, batch, seq_len, *, tm=512, tk=DIFF_TK):
    n = x.shape[0]
    tiles_per_seq = seq_len // tm
    return pl.pallas_call(
        _proj_kernel,
        grid=(n // tm,),
        in_specs=[
            pl.BlockSpec((tm, D_MODEL), lambda i: (i, 0)),
            pl.BlockSpec((1, D_MODEL), lambda i: (0, 0)),
            pl.BlockSpec((D_MODEL, IN_WIDTH), lambda i: (0, 0)),
            pl.BlockSpec((4, PROJ_BLOCK), lambda i: (0, 0)),
            pl.BlockSpec((PROJ_BLOCK, PROJ_BLOCK), lambda i: (0, 0)),
            pl.BlockSpec((tm, V7X_LANES), lambda i: (i % tiles_per_seq, 0)),
            pl.BlockSpec((tm, V7X_LANES), lambda i: (i % tiles_per_seq, 0)),
            pl.BlockSpec((tm, V7X_LANES), lambda i: (i % tiles_per_seq, 0)),
        ],
        out_specs=[
            pl.BlockSpec((tm, PROJ_OUT), lambda i: (i, 0)),
            pl.BlockSpec((None, DIFF_HEADS, tm // tk, DIFF_V_DIM, tk),
                         lambda i: (i // tiles_per_seq, 0, i % tiles_per_seq, 0, 0)),
        ],
        out_shape=[
            jax.ShapeDtypeStruct((n, PROJ_OUT), BF16),
            jax.ShapeDtypeStruct((batch, DIFF_HEADS, seq_len // tk, DIFF_V_DIM, tk), BF16),
        ],
        compiler_params=_params("parallel"),
        name="in_proj",
    )(x, g, w, head_gains, blockdiag, cos, sa, sb)


def _natten_kernel(q_ref, k_ref, v_ref, bias_ref, o_ref, *, rows, rows_per_step):
    n_keys = WIN_H * GRID_W
    r0 = pl.program_id(1) * rows_per_step
    block_start = jnp.clip(r0 - WIN_H // 2, 0, rows - (rows_per_step + WIN_H - 1))
    lane = lax.broadcasted_iota(jnp.int32, (GRID_W, V7X_LANES), 1)
    lo = lane < HEAD_DIM
    tasks = [(i, pair) for i in range(rows_per_step) for pair in range(NA_HEADS // 2)]

    def key_rows(i):
        start = jnp.clip(r0 + i - WIN_H // 2, 0, rows - WIN_H)
        variant = start - (r0 + i) + WIN_H - 1
        return pl.ds(pl.multiple_of((start - block_start) * GRID_W, GRID_W), n_keys), variant

    def scores(i, pair):
        keys, variant = key_rows(i)
        cols = slice(pair * V7X_LANES, (pair + 1) * V7X_LANES)
        qp = q_ref[i * GRID_W:(i + 1) * GRID_W, cols]
        zero = jnp.zeros_like(qp)
        lhs = jnp.concatenate([jnp.where(lo, qp, zero), jnp.where(lo, zero, qp)], axis=0)
        s = lax.dot_general(lhs, k_ref[keys, cols], (((1,), (1,)), ((), ())), preferred_element_type=F32)
        return s + bias_ref[variant, pair * 2 * GRID_W:(pair + 1) * 2 * GRID_W, :]

    s_next = scores(*tasks[0])
    for n, (i, pair) in enumerate(tasks):
        s = s_next
        if n + 1 < len(tasks):
            s_next = scores(*tasks[n + 1])
        keys, _ = key_rows(i)
        cols = slice(pair * V7X_LANES, (pair + 1) * V7X_LANES)
        m = jnp.max(s, axis=-1, keepdims=True)
        e = jnp.exp(s - m)
        l = jnp.sum(e, axis=-1, keepdims=True)
        o = jnp.dot(e.astype(BF16), v_ref[keys, cols], preferred_element_type=F32)
        o = o * (1.0 / l)
        o_ref[i * GRID_W:(i + 1) * GRID_W, cols] = jnp.where(lo, o[:GRID_W], o[GRID_W:]).astype(o_ref.dtype)


def _natten(proj, bias, batch, seq_len, *, rows_per_step=4):
    rows = seq_len // GRID_W
    n = proj.shape[0]
    block_rows = rows_per_step + WIN_H - 1
    assert rows % rows_per_step == 0 and rows >= block_rows
    steps = rows // rows_per_step

    def block_start(j):
        return jnp.clip(j * rows_per_step - WIN_H // 2, 0, rows - block_rows)

    return pl.pallas_call(
        functools.partial(_natten_kernel, rows=rows, rows_per_step=rows_per_step),
        grid=(batch, steps),
        in_specs=[
            pl.BlockSpec((rows_per_step * GRID_W, NA_WIDTH), lambda b, j: (b * steps + j, QA_OFF // NA_WIDTH)),
            pl.BlockSpec((pl.Element(block_rows * GRID_W), pl.Element(NA_WIDTH)),
                         lambda b, j: ((b * rows + block_start(j)) * GRID_W, KA_OFF)),
            pl.BlockSpec((pl.Element(block_rows * GRID_W), pl.Element(NA_WIDTH)),
                         lambda b, j: ((b * rows + block_start(j)) * GRID_W, VA_OFF)),
            pl.BlockSpec((WIN_H, NA_HEADS * GRID_W, WIN_H * GRID_W), lambda b, j: (0, 0, 0)),
        ],
        out_specs=pl.BlockSpec((rows_per_step * GRID_W, NA_WIDTH), lambda b, j: (b * steps + j, 0)),
        out_shape=jax.ShapeDtypeStruct((n, NA_WIDTH), BF16),
        compiler_params=_params("parallel", "arbitrary"),
        name="natten",
    )(proj, proj, proj, bias)


def _natten_bias(rpb):
    qc = np.arange(GRID_W)[:, None]
    kc = np.arange(GRID_W)[None, :]
    wstart = np.clip(qc - WIN_W // 2, 0, GRID_W - WIN_W)
    in_win = (kc >= wstart) & (kc < wstart + WIN_W)
    dc = np.clip(kc - qc + WIN_W - 1, 0, 2 * WIN_W - 2)
    dr = np.arange(WIN_H)[:, None] + np.arange(WIN_H)[None, :]
    b = rpb.astype(F32)[:, dr][:, :, :, dc]
    b = jnp.where(in_win[None, None, None], b, NEG_INF)
    b = b.transpose(1, 0, 3, 2, 4)
    return b.reshape(WIN_H, NA_HEADS * GRID_W, WIN_H * GRID_W)


def _diff_kernel(sc_ref, q_ref, k_ref, vt_ref, g_ref, o_ref,
                 qst_scr, s_a, s_b, p_a, p_b, al_a, al_b, m_scr, l_scr, acc_scr, *, tk):
    tq = q_ref.shape[0]
    n_chunks = k_ref.shape[0] // tk
    qt = q_ref[...].astype(F32).T
    row = lax.broadcasted_iota(jnp.int32, qt.shape, 0)
    zero = jnp.zeros_like(qt)
    qst_scr[...] = jnp.concatenate(
        [jnp.where(row < HEAD_DIM, qt, zero), jnp.where(row < HEAD_DIM, zero, qt)], axis=1).astype(BF16)
    m_scr[...] = jnp.full(m_scr.shape, NEG_INF, F32)
    l_scr[...] = jnp.zeros(l_scr.shape, F32)
    acc_scr[...] = jnp.zeros(acc_scr.shape, F32)

    def scores(c, s_ref):
        start = pl.multiple_of(c * tk, tk)
        s_ref[...] = jnp.dot(k_ref[pl.ds(start, tk), :], qst_scr[...], preferred_element_type=F32)

    def softmax(s_ref, p_ref, al_ref):
        s = s_ref[...]
        m_prev = m_scr[...]
        m_new = jnp.maximum(m_prev, jnp.max(s, axis=0, keepdims=True))
        alpha = jnp.exp2(m_prev - m_new)
        p = jnp.exp2(s - m_new)
        l_scr[...] = alpha * l_scr[...] + jnp.sum(p, axis=0, keepdims=True)
        m_scr[...] = m_new
        al_ref[...] = alpha
        p_ref[...] = p.astype(BF16)

    def accumulate(c, p_ref, al_ref):
        acc_scr[...] = al_ref[...] * acc_scr[...] + jnp.dot(vt_ref[c], p_ref[...], preferred_element_type=F32)

    def pair(c, first, last):
        if not first:
            accumulate(c - 1, p_b, al_b)
        softmax(s_a, p_a, al_a)
        scores(c + 1, s_b)
        accumulate(c, p_a, al_a)
        softmax(s_b, p_b, al_b)
        if not last:
            scores(c + 2, s_a)

    scores(0, s_a)
    pair(0, True, n_chunks == 2)
    if n_chunks > 4:
        def body(i, carry):
            pair(2 * i, False, False)
            return carry
        lax.fori_loop(1, n_chunks // 2 - 1, body, 0)
    if n_chunks > 2:
        pair(n_chunks - 2, False, True)
    accumulate(n_chunks - 1, p_b, al_b)

    o = acc_scr[...] * (1.0 / l_scr[...])
    d = o[:, :tq] - sc_ref[0] * o[:, tq:]
    ms = jnp.mean(d * d, axis=0, keepdims=True)
    y = d * lax.rsqrt(ms + SUBLN_EPS) * (g_ref[...] * sc_ref[1])
    o_ref[...] = y.T.astype(o_ref.dtype)


def _diffattn(proj, vt, scalars, subln_col, batch, seq_len, *, tq=256, tk=256):
    n = proj.shape[0]
    q_tiles = seq_len // tq
    assert (seq_len // tk) % 2 == 0
    return pl.pallas_call(
        functools.partial(_diff_kernel, tk=tk),
        grid=(batch, DIFF_HEADS, q_tiles),
        in_specs=[
            pl.BlockSpec(memory_space=pltpu.SMEM),
            pl.BlockSpec((tq, DIFF_V_DIM), lambda b, h, i: (b * q_tiles + i, QB_OFF // DIFF_V_DIM + h)),
            pl.BlockSpec((seq_len, DIFF_V_DIM), lambda b, h, i: (b, KB_OFF // DIFF_V_DIM + h)),
            pl.BlockSpec((None, None, seq_len // tk, DIFF_V_DIM, tk), lambda b, h, i: (b, h, 0, 0, 0)),
            pl.BlockSpec((DIFF_V_DIM, 1), lambda b, h, i: (0, 0)),
        ],
        out_specs=pl.BlockSpec((tq, DIFF_V_DIM), lambda b, h, i: (b * q_tiles + i, h)),
        out_shape=jax.ShapeDtypeStruct((n, DIFF_HEADS * DIFF_V_DIM), BF16),
        scratch_shapes=[
            pltpu.VMEM((DIFF_V_DIM, 2 * tq), BF16),
            pltpu.VMEM((tk, 2 * tq), F32),
            pltpu.VMEM((tk, 2 * tq), F32),
            pltpu.VMEM((tk, 2 * tq), BF16),
            pltpu.VMEM((tk, 2 * tq), BF16),
            pltpu.VMEM((1, 2 * tq), F32),
            pltpu.VMEM((1, 2 * tq), F32),
            pltpu.VMEM((1, 2 * tq), F32),
            pltpu.VMEM((1, 2 * tq), F32),
            pltpu.VMEM((DIFF_V_DIM, 2 * tq), F32),
        ],
        compiler_params=_params("parallel", "parallel", "arbitrary"),
        name="diffattn",
    )(scalars, proj, proj, vt, subln_col)


def _merge_kernel(x_ref, ya_ref, yb_ref, ga_ref, gb_ref, wa_ref, wb_ref, wo_ref, o_ref):
    a = jnp.dot(ya_ref[...], wa_ref[...], preferred_element_type=F32)
    b = jnp.dot(yb_ref[...], wb_ref[...], preferred_element_type=F32)
    merged = ga_ref[...].astype(F32) * a + gb_ref[...].astype(F32) * b
    o_ref[...] = x_ref[...] + jnp.dot(merged.astype(BF16), wo_ref[...], preferred_element_type=F32)


def _merge(x, ya, yb, proj, wa, wb, wo, *, tm=512):
    n = x.shape[0]
    return pl.pallas_call(
        _merge_kernel,
        grid=(n // tm,),
        in_specs=[
            pl.BlockSpec((tm, D_MODEL), lambda i: (i, 0)),
            pl.BlockSpec((tm, NA_WIDTH), lambda i: (i, 0)),
            pl.BlockSpec((tm, NA_WIDTH), lambda i: (i, 0)),
            pl.BlockSpec((tm, D_MODEL), lambda i: (i, GA_OFF // D_MODEL)),
            pl.BlockSpec((tm, D_MODEL), lambda i: (i, GB_OFF // D_MODEL)),
            pl.BlockSpec((NA_WIDTH, D_MODEL), lambda i: (0, 0)),
            pl.BlockSpec((NA_WIDTH, D_MODEL), lambda i: (0, 0)),
            pl.BlockSpec((D_MODEL, D_MODEL), lambda i: (0, 0)),
        ],
        out_specs=pl.BlockSpec((tm, D_MODEL), lambda i: (i, 0)),
        out_shape=jax.ShapeDtypeStruct((n, D_MODEL), F32),
        compiler_params=_params("parallel"),
        name="merge",
    )(x, ya, yb, proj, proj, wa, wb, wo)


def _rope_tables(seq_len):
    half = ROPE_DIM // 2
    inv_freq = jnp.power(ROPE_THETA, -jnp.arange(half, dtype=F32) * 2.0 / ROPE_DIM)
    ang = jnp.arange(seq_len, dtype=F32)[:, None] * inv_freq[None, :]
    cos, sin = jnp.cos(ang), jnp.sin(ang)
    ones = jnp.ones((seq_len, HEAD_DIM - ROPE_DIM), F32)
    zeros = jnp.zeros((seq_len, HEAD_DIM - ROPE_DIM), F32)
    zh = jnp.zeros_like(sin)
    c = jnp.concatenate([cos, cos, ones], axis=1)
    sa = jnp.concatenate([-sin, zh, zeros], axis=1)
    sb = jnp.concatenate([zh, sin, zeros], axis=1)
    reps = V7X_LANES // HEAD_DIM
    return tuple(jnp.tile(t, (1, reps)) for t in (c, sa, sb))


def _layer_consts(l, w):
    scale = HEAD_DIM ** -0.5
    heads = PROJ_BLOCK // HEAD_DIM
    head_gains = jnp.stack([
        jnp.tile(w["qa_norm"][l] * scale, heads), jnp.tile(w["ka_norm"][l], heads),
        jnp.tile(w["qb_norm"][l] * (scale * LOG2_E), heads), jnp.tile(w["kb_norm"][l], heads)])
    blk = np.arange(PROJ_BLOCK) // HEAD_DIM
    blockdiag = jnp.asarray((blk[:, None] == blk[None, :]).astype(np.float32) / HEAD_DIM, BF16)
    lam_init = 0.8 - 0.6 * math.exp(-0.3 * l)
    lam = (jnp.exp(jnp.sum(w["lam_q1"][l] * w["lam_k1"][l]))
           - jnp.exp(jnp.sum(w["lam_q2"][l] * w["lam_k2"][l])) + lam_init)
    return dict(
        ffn1_g=w["ffn1_norm"][l][None], ffn1_wi=w["ffn1_wi"][l].astype(BF16), ffn1_wo=w["ffn1_wo"][l].astype(BF16),
        mix_g=w["mix_norm"][l][None], w_in=w["w_in"][l].astype(BF16),
        head_gains=head_gains, blockdiag=blockdiag, bias=_natten_bias(w["rpb"][l]),
        scalars=jnp.stack([lam, jnp.asarray(1.0 - lam_init, F32)]).astype(F32),
        subln=w["subln"][l][:, None],
        w_a=w["w_a_out"][l].astype(BF16), w_b=w["w_b_out"][l].astype(BF16), w_o=w["w_o"][l].astype(BF16),
        ffn2_g=w["ffn2_norm"][l][None], ffn2_wi=w["ffn2_wi"][l].astype(BF16), ffn2_wo=w["ffn2_wo"][l].astype(BF16),
    )


def _encoder_layer(x, c, rope, batch, seq_len):
    x = _ffn(x, c["ffn1_g"], c["ffn1_wi"], c["ffn1_wo"])
    proj, vt = _in_proj(x, c["mix_g"], c["w_in"], c["head_gains"], c["blockdiag"], *rope, batch, seq_len)
    ya = _natten(proj, c["bias"], batch, seq_len)
    yb = _diffattn(proj, vt, c["scalars"], c["subln"], batch, seq_len, tk=DIFF_TK)
    x = _merge(x, ya, yb, proj, c["w_a"], c["w_b"], c["w_o"])
    return _ffn(x, c["ffn2_g"], c["ffn2_wi"], c["ffn2_wo"])


def _encoder(groups, weights, depth):
    consts = [_layer_consts(l, weights) for l in range(depth)]
    outs = []
    for x in groups:
        batch, seq_len, _ = x.shape
        rope = _rope_tables(seq_len)
        y = x.reshape(batch * seq_len, D_MODEL)
        for c in consts:
            y = _encoder_layer(y, c, rope, batch, seq_len)
        outs.append(y.reshape(x.shape))
    return tuple(outs)


def kernel(x_prompt, x_sample, ffn1_norm, ffn1_wi, ffn1_wo, mix_norm, w_in, qa_norm, ka_norm, rpb, qb_norm, kb_norm, lam_q1, lam_k1, lam_q2, lam_k2, subln, w_a_out, w_b_out, w_o, ffn2_norm, ffn2_wi, ffn2_wo):
    weights = dict(ffn1_norm=ffn1_norm, ffn1_wi=ffn1_wi, ffn1_wo=ffn1_wo, mix_norm=mix_norm, w_in=w_in,
                   qa_norm=qa_norm, ka_norm=ka_norm, rpb=rpb, qb_norm=qb_norm, kb_norm=kb_norm,
                   lam_q1=lam_q1, lam_k1=lam_k1, lam_q2=lam_q2, lam_k2=lam_k2, subln=subln,
                   w_a_out=w_a_out, w_b_out=w_b_out, w_o=w_o,
                   ffn2_norm=ffn2_norm, ffn2_wi=ffn2_wi, ffn2_wo=ffn2_wo)
    return _encoder((x_prompt, x_sample), weights, ffn1_norm.shape[0])
```

```python
import functools
import math

import jax
import jax.numpy as jnp
import numpy as np
from jax import lax
from jax.experimental import pallas as pl
from jax.experimental.pallas import tpu as pltpu

F32 = jnp.float32
BF16 = jnp.bfloat16

D_MODEL = 1024
GRID_W = 64
HEAD_DIM = 64
NA_HEADS = 8
DIFF_HEADS = 4
NA_WIDTH = NA_HEADS * HEAD_DIM
DIFF_V_DIM = 2 * HEAD_DIM
IN_WIDTH = 5120
D_FF = 2816
WIN_H = 8
WIN_W = 16
ROPE_THETA = 500000.0
ROPE_DIM = HEAD_DIM // 4
NORM_EPS = 1e-6
SUBLN_EPS = 1e-5
NEG_INF = -1e30
FFN_RES = 0.5

V7X_VMEM_BYTES = 64 * 1024 * 1024
V7X_LANES = 128
VMEM_LIMIT = V7X_VMEM_BYTES * 7 // 8

PROJ_BLOCK = 512
W_QA, W_KA, W_VA, W_QB, W_KB, W_VB, W_GA, W_GB = 0, 512, 1024, 1536, 2048, 2560, 3072, 4096
GA_OFF, GB_OFF, QA_OFF, KA_OFF, VA_OFF, QB_OFF, KB_OFF = 0, 1024, 2048, 2560, 3072, 3584, 4096
PROJ_OUT = 4608
DIFF_TK = 512
BF16_SUBLANES = 16
DIFF_VT_ROWS = DIFF_V_DIM + BF16_SUBLANES
LOG2_E = math.log2(math.e)


def _params(*semantics):
    return pltpu.CompilerParams(dimension_semantics=semantics, vmem_limit_bytes=VMEM_LIMIT)


def _rms(x, g, eps):
    return x * lax.rsqrt(jnp.mean(x * x, axis=-1, keepdims=True) + eps) * g


def _ffn_kernel(x_ref, g_ref, wi_ref, wo_ref, o_ref, *, chunk):
    x = x_ref[...]
    h = _rms(x, g_ref[...], NORM_EPS).astype(BF16)
    acc = jnp.zeros(x.shape, F32)
    for c in range(D_FF // chunk):
        a = jnp.dot(h, wi_ref[:, c * chunk:(c + 1) * chunk], preferred_element_type=F32)
        b = jnp.dot(h, wi_ref[:, D_FF + c * chunk:D_FF + (c + 1) * chunk], preferred_element_type=F32)
        gate = (a * jax.nn.sigmoid(a) * b).astype(BF16)
        acc = acc + jnp.dot(gate, wo_ref[c * chunk:(c + 1) * chunk, :], preferred_element_type=F32)
    o_ref[...] = x + FFN_RES * acc


def _ffn(x, g, wi, wo, *, tm=512, chunk=1408):
    n = x.shape[0]
    return pl.pallas_call(
        functools.partial(_ffn_kernel, chunk=chunk),
        grid=(n // tm,),
        in_specs=[
            pl.BlockSpec((tm, D_MODEL), lambda i: (i, 0)),
            pl.BlockSpec((1, D_MODEL), lambda i: (0, 0)),
            pl.BlockSpec((D_MODEL, 2 * D_FF), lambda i: (0, 0)),
            pl.BlockSpec((D_FF, D_MODEL), lambda i: (0, 0)),
        ],
        out_specs=pl.BlockSpec((tm, D_MODEL), lambda i: (i, 0)),
        out_shape=jax.ShapeDtypeStruct((n, D_MODEL), F32),
        compiler_params=_params("parallel"),
        name="ffn",
    )(x, g, wi, wo)


def _proj_kernel(x_ref, g_ref, w_ref, hg_ref, bd_ref, cos_ref, sa_ref, sb_ref, o_ref, vt_ref):
    h = _rms(x_ref[...], g_ref[...], NORM_EPS).astype(BF16)
    tk = vt_ref.shape[-1]

    def proj(w_off):
        return jnp.dot(h, w_ref[:, w_off:w_off + PROJ_BLOCK], preferred_element_type=F32)

    def head_norm(p, gain_row):
        ms = jnp.dot((p * p).astype(BF16), bd_ref[...], preferred_element_type=F32)
        return p * lax.rsqrt(ms + NORM_EPS) * hg_ref[gain_row:gain_row + 1, :]

    reps = PROJ_BLOCK // V7X_LANES
    cos = jnp.concatenate([cos_ref[...]] * reps, axis=1)
    sa = jnp.concatenate([sa_ref[...]] * reps, axis=1)
    sb = jnp.concatenate([sb_ref[...]] * reps, axis=1)

    def rope(y):
        half = ROPE_DIM // 2
        up = pltpu.roll(y, PROJ_BLOCK - half, 1)
        dn = pltpu.roll(y, half, 1)
        return y * cos + up * sa + dn * sb

    def put(off, val):
        o_ref[:, off:off + PROJ_BLOCK] = val.astype(o_ref.dtype)

    put(QA_OFF, head_norm(proj(W_QA), 0))
    put(KA_OFF, head_norm(proj(W_KA), 1))
    put(VA_OFF, proj(W_VA))
    put(QB_OFF, rope(head_norm(proj(W_QB), 2)))
    put(KB_OFF, rope(head_norm(proj(W_KB), 3)))
    vb = proj(W_VB)
    pad_row = lax.broadcasted_iota(jnp.int32, (BF16_SUBLANES, tk), 0)
    ones_rows = jnp.where(pad_row == 0, 1.0, 0.0).astype(vt_ref.dtype)
    for head in range(DIFF_HEADS):
        for c in range(vt_ref.shape[1]):
            blk = vb[c * tk:(c + 1) * tk, head * DIFF_V_DIM:(head + 1) * DIFF_V_DIM]
            vt_ref[head, c, :DIFF_V_DIM, :] = blk.T.astype(vt_ref.dtype)
            vt_ref[head, c, DIFF_V_DIM:, :] = ones_rows
    for half in range(D_MODEL // PROJ_BLOCK):
        put(GA_OFF + half * PROJ_BLOCK, jax.nn.sigmoid(proj(W_GA + half * PROJ_BLOCK)))
        put(GB_OFF + half * PROJ_BLOCK, jax.nn.sigmoid(proj(W_GB + half * PROJ_BLOCK)))


def _in_proj(x, g, w, head_gains, blockdiag, cos, sa, sb, batch, seq_len, *, tm=512, tk=DIFF_TK):
    n = x.shape[0]
    tiles_per_seq = seq_len // tm
    return pl.pallas_call(
        _proj_kernel,
        grid=(n // tm,),
        in_specs=[
            pl.BlockSpec((tm, D_MODEL), lambda i: (i, 0)),
            pl.BlockSpec((1, D_MODEL), lambda i: (0, 0)),
            pl.BlockSpec((D_MODEL, IN_WIDTH), lambda i: (0, 0)),
            pl.BlockSpec((4, PROJ_BLOCK), lambda i: (0, 0)),
            pl.BlockSpec((PROJ_BLOCK, PROJ_BLOCK), lambda i: (0, 0)),
            pl.BlockSpec((tm, V7X_LANES), lambda i: (i % tiles_per_seq, 0)),
            pl.BlockSpec((tm, V7X_LANES), lambda i: (i % tiles_per_seq, 0)),
            pl.BlockSpec((tm, V7X_LANES), lambda i: (i % tiles_per_seq, 0)),
        ],
        out_specs=[
            pl.BlockSpec((tm, PROJ_OUT), lambda i: (i, 0)),
            pl.BlockSpec((None, DIFF_HEADS, tm // tk, DIFF_VT_ROWS, tk),
                         lambda i: (i // tiles_per_seq, 0, i % tiles_per_seq, 0, 0)),
        ],
        out_shape=[
            jax.ShapeDtypeStruct((n, PROJ_OUT), BF16),
            jax.ShapeDtypeStruct((batch, DIFF_HEADS, seq_len // tk, DIFF_VT_ROWS, tk), BF16),
        ],
        compiler_params=_params("parallel"),
        name="in_proj",
    )(x, g, w, head_gains, blockdiag, cos, sa, sb)


def _natten_kernel(q_ref, k_ref, v_ref, bias_ref, o_ref, *, rows, rows_per_step):
    n_keys = WIN_H * GRID_W
    r0 = pl.program_id(1) * rows_per_step
    block_start = jnp.clip(r0 - WIN_H // 2, 0, rows - (rows_per_step + WIN_H - 1))
    lane = lax.broadcasted_iota(jnp.int32, (GRID_W, V7X_LANES), 1)
    lo = lane < HEAD_DIM
    tasks = [(i, pair) for i in range(rows_per_step) for pair in range(NA_HEADS // 2)]

    def key_rows(i):
        start = jnp.clip(r0 + i - WIN_H // 2, 0, rows - WIN_H)
        variant = start - (r0 + i) + WIN_H - 1
        return pl.ds(pl.multiple_of((start - block_start) * GRID_W, GRID_W), n_keys), variant

    def scores(i, pair):
        keys, variant = key_rows(i)
        cols = slice(pair * V7X_LANES, (pair + 1) * V7X_LANES)
        qp = q_ref[i * GRID_W:(i + 1) * GRID_W, cols]
        zero = jnp.zeros_like(qp)
        lhs = jnp.concatenate([jnp.where(lo, qp, zero), jnp.where(lo, zero, qp)], axis=0)
        s = lax.dot_general(lhs, k_ref[keys, cols], (((1,), (1,)), ((), ())), preferred_element_type=F32)
        return s + bias_ref[variant, pair * 2 * GRID_W:(pair + 1) * 2 * GRID_W, :]

    s_next = scores(*tasks[0])
    for n, (i, pair) in enumerate(tasks):
        s = s_next
        if n + 1 < len(tasks):
            s_next = scores(*tasks[n + 1])
        keys, _ = key_rows(i)
        cols = slice(pair * V7X_LANES, (pair + 1) * V7X_LANES)
        m = jnp.max(s, axis=-1, keepdims=True)
        e = jnp.exp(s - m)
        l = jnp.sum(e, axis=-1, keepdims=True)
        o = jnp.dot(e.astype(BF16), v_ref[keys, cols], preferred_element_type=F32)
        o = o * (1.0 / l)
        o_ref[i * GRID_W:(i + 1) * GRID_W, cols] = jnp.where(lo, o[:GRID_W], o[GRID_W:]).astype(o_ref.dtype)


def _natten(proj, bias, batch, seq_len, *, rows_per_step=4):
    rows = seq_len // GRID_W
    n = proj.shape[0]
    block_rows = rows_per_step + WIN_H - 1
    assert rows % rows_per_step == 0 and rows >= block_rows
    steps = rows // rows_per_step

    def block_start(j):
        return jnp.clip(j * rows_per_step - WIN_H // 2, 0, rows - block_rows)

    return pl.pallas_call(
        functools.partial(_natten_kernel, rows=rows, rows_per_step=rows_per_step),
        grid=(batch, steps),
        in_specs=[
            pl.BlockSpec((rows_per_step * GRID_W, NA_WIDTH), lambda b, j: (b * steps + j, QA_OFF // NA_WIDTH)),
            pl.BlockSpec((pl.Element(block_rows * GRID_W), pl.Element(NA_WIDTH)),
                         lambda b, j: ((b * rows + block_start(j)) * GRID_W, KA_OFF)),
            pl.BlockSpec((pl.Element(block_rows * GRID_W), pl.Element(NA_WIDTH)),
                         lambda b, j: ((b * rows + block_start(j)) * GRID_W, VA_OFF)),
            pl.BlockSpec((WIN_H, NA_HEADS * GRID_W, WIN_H * GRID_W), lambda b, j: (0, 0, 0)),
        ],
        out_specs=pl.BlockSpec((rows_per_step * GRID_W, NA_WIDTH), lambda b, j: (b * steps + j, 0)),
        out_shape=jax.ShapeDtypeStruct((n, NA_WIDTH), BF16),
        compiler_params=_params("parallel", "arbitrary"),
        name="natten",
    )(proj, proj, proj, bias)


def _natten_bias(rpb):
    qc = np.arange(GRID_W)[:, None]
    kc = np.arange(GRID_W)[None, :]
    wstart = np.clip(qc - WIN_W // 2, 0, GRID_W - WIN_W)
    in_win = (kc >= wstart) & (kc < wstart + WIN_W)
    dc = np.clip(kc - qc + WIN_W - 1, 0, 2 * WIN_W - 2)
    dr = np.arange(WIN_H)[:, None] + np.arange(WIN_H)[None, :]
    b = rpb.astype(F32)[:, dr][:, :, :, dc]
    b = jnp.where(in_win[None, None, None], b, NEG_INF)
    b = b.transpose(1, 0, 3, 2, 4)
    return b.reshape(WIN_H, NA_HEADS * GRID_W, WIN_H * GRID_W)


def _diff_kernel(sc_ref, q_ref, k_ref, vt_ref, g_ref, o_ref,
                 qst_scr, s_a, s_b, cm_a, cm_b, p_a, p_b, al_a, al_b, m_scr, acc_scr, *, tk):
    tq = q_ref.shape[0]
    n_chunks = k_ref.shape[0] // tk
    qt = q_ref[...].astype(F32).T
    row = lax.broadcasted_iota(jnp.int32, qt.shape, 0)
    zero = jnp.zeros_like(qt)
    qst_scr[...] = jnp.concatenate(
        [jnp.where(row < HEAD_DIM, qt, zero), jnp.where(row < HEAD_DIM, zero, qt)], axis=1).astype(BF16)
    m_scr[...] = jnp.full(m_scr.shape, NEG_INF, F32)
    acc_scr[...] = jnp.zeros(acc_scr.shape, F32)

    def scores(c, s_ref, cm_ref):
        start = pl.multiple_of(c * tk, tk)
        s = jnp.dot(k_ref[pl.ds(start, tk), :], qst_scr[...], preferred_element_type=F32)
        s_ref[...] = s
        cm_ref[...] = jnp.max(s, axis=0, keepdims=True)

    def softmax(s_ref, cm_ref, p_ref, al_ref):
        m_prev = m_scr[...]
        m_new = jnp.maximum(m_prev, cm_ref[...])
        al_ref[...] = jnp.exp2(m_prev - m_new)
        m_scr[...] = m_new
        p_ref[...] = jnp.exp2(s_ref[...] - m_new).astype(BF16)

    def accumulate(c, p_ref, al_ref):
        acc_scr[...] = al_ref[...] * acc_scr[...] + jnp.dot(vt_ref[c], p_ref[...], preferred_element_type=F32)

    def pair(c, first, last):
        scores(c + 1, s_b, cm_b)
        softmax(s_a, cm_a, p_a, al_a)
        if not first:
            accumulate(c - 1, p_b, al_b)
        if not last:
            scores(c + 2, s_a, cm_a)
        softmax(s_b, cm_b, p_b, al_b)
        accumulate(c, p_a, al_a)

    scores(0, s_a, cm_a)
    pair(0, True, n_chunks == 2)
    inner_pairs = n_chunks // 2 - 2
    if inner_pairs > 0:
        per_trip = 2 if inner_pairs % 2 == 0 else 1
        if inner_pairs == per_trip:
            for j in range(per_trip):
                pair(2 * (1 + j), False, False)
        else:
            def body(i, carry):
                for j in range(per_trip):
                    pair(2 * (1 + i * per_trip + j), False, False)
                return carry
            lax.fori_loop(0, inner_pairs // per_trip, body, 0)
    if n_chunks > 2:
        pair(n_chunks - 2, False, True)
    accumulate(n_chunks - 1, p_b, al_b)

    acc = acc_scr[...]
    o = acc[:DIFF_V_DIM] * (1.0 / acc[DIFF_V_DIM:DIFF_V_DIM + 1])
    d = o[:, :tq] - sc_ref[0] * o[:, tq:]
    ms = jnp.mean(d * d, axis=0, keepdims=True)
    y = d * lax.rsqrt(ms + SUBLN_EPS) * (g_ref[...] * sc_ref[1])
    o_ref[...] = y.T.astype(o_ref.dtype)


def _diffattn(proj, vt, scalars, subln_col, batch, seq_len, *, tq=512):
    n = proj.shape[0]
    tk = vt.shape[-1]
    q_tiles = seq_len // tq
    assert (seq_len // tk) % 2 == 0
    return pl.pallas_call(
        functools.partial(_diff_kernel, tk=tk),
        grid=(batch, DIFF_HEADS, q_tiles),
        in_specs=[
            pl.BlockSpec(memory_space=pltpu.SMEM),
            pl.BlockSpec((tq, DIFF_V_DIM), lambda b, h, i: (b * q_tiles + i, QB_OFF // DIFF_V_DIM + h)),
            pl.BlockSpec((seq_len, DIFF_V_DIM), lambda b, h, i: (b, KB_OFF // DIFF_V_DIM + h)),
            pl.BlockSpec((None, None, seq_len // tk, DIFF_VT_ROWS, tk), lambda b, h, i: (b, h, 0, 0, 0)),
            pl.BlockSpec((DIFF_V_DIM, 1), lambda b, h, i: (0, 0)),
        ],
        out_specs=pl.BlockSpec((tq, DIFF_V_DIM), lambda b, h, i: (b * q_tiles + i, h)),
        out_shape=jax.ShapeDtypeStruct((n, DIFF_HEADS * DIFF_V_DIM), BF16),
        scratch_shapes=[
            pltpu.VMEM((DIFF_V_DIM, 2 * tq), BF16),
            pltpu.VMEM((tk, 2 * tq), F32),
            pltpu.VMEM((tk, 2 * tq), F32),
            pltpu.VMEM((1, 2 * tq), F32),
            pltpu.VMEM((1, 2 * tq), F32),
            pltpu.VMEM((tk, 2 * tq), BF16),
            pltpu.VMEM((tk, 2 * tq), BF16),
            pltpu.VMEM((1, 2 * tq), F32),
            pltpu.VMEM((1, 2 * tq), F32),
            pltpu.VMEM((1, 2 * tq), F32),
            pltpu.VMEM((DIFF_VT_ROWS, 2 * tq), F32),
        ],
        compiler_params=_params("parallel", "parallel", "arbitrary"),
        name="diffattn",
    )(scalars, proj, proj, vt, subln_col)


def _merge_kernel(x_ref, ya_ref, yb_ref, ga_ref, gb_ref, wa_ref, wb_ref, wo_ref, o_ref):
    a = jnp.dot(ya_ref[...], wa_ref[...], preferred_element_type=F32)
    b = jnp.dot(yb_ref[...], wb_ref[...], preferred_element_type=F32)
    merged = ga_ref[...].astype(F32) * a + gb_ref[...].astype(F32) * b
    o_ref[...] = x_ref[...] + jnp.dot(merged.astype(BF16), wo_ref[...], preferred_element_type=F32)


def _merge(x, ya, yb, proj, wa, wb, wo, *, tm=512):
    n = x.shape[0]
    return pl.pallas_call(
        _merge_kernel,
        grid=(n // tm,),
        in_specs=[
            pl.BlockSpec((tm, D_MODEL), lambda i: (i, 0)),
            pl.BlockSpec((tm, NA_WIDTH), lambda i: (i, 0)),
            pl.BlockSpec((tm, NA_WIDTH), lambda i: (i, 0)),
            pl.BlockSpec((tm, D_MODEL), lambda i: (i, GA_OFF // D_MODEL)),
            pl.BlockSpec((tm, D_MODEL), lambda i: (i, GB_OFF // D_MODEL)),
            pl.BlockSpec((NA_WIDTH, D_MODEL), lambda i: (0, 0)),
            pl.BlockSpec((NA_WIDTH, D_MODEL), lambda i: (0, 0)),
            pl.BlockSpec((D_MODEL, D_MODEL), lambda i: (0, 0)),
        ],
        out_specs=pl.BlockSpec((tm, D_MODEL), lambda i: (i, 0)),
        out_shape=jax.ShapeDtypeStruct((n, D_MODEL), F32),
        compiler_params=_params("parallel"),
        name="merge",
    )(x, ya, yb, proj, proj, wa, wb, wo)


def _rope_tables(seq_len):
    half = ROPE_DIM // 2
    inv_freq = jnp.power(ROPE_THETA, -jnp.arange(half, dtype=F32) * 2.0 / ROPE_DIM)
    ang = jnp.arange(seq_len, dtype=F32)[:, None] * inv_freq[None, :]
    cos, sin = jnp.cos(ang), jnp.sin(ang)
    ones = jnp.ones((seq_len, HEAD_DIM - ROPE_DIM), F32)
    zeros = jnp.zeros((seq_len, HEAD_DIM - ROPE_DIM), F32)
    zh = jnp.zeros_like(sin)
    c = jnp.concatenate([cos, cos, ones], axis=1)
    sa = jnp.concatenate([-sin, zh, zeros], axis=1)
    sb = jnp.concatenate([zh, sin, zeros], axis=1)
    reps = V7X_LANES // HEAD_DIM
    return tuple(jnp.tile(t, (1, reps)) for t in (c, sa, sb))


def _layer_consts(l, w):
    scale = HEAD_DIM ** -0.5
    heads = PROJ_BLOCK // HEAD_DIM
    head_gains = jnp.stack([
        jnp.tile(w["qa_norm"][l] * scale, heads), jnp.tile(w["ka_norm"][l], heads),
        jnp.tile(w["qb_norm"][l] * (scale * LOG2_E), heads), jnp.tile(w["kb_norm"][l], heads)])
    blk = np.arange(PROJ_BLOCK) // HEAD_DIM
    blockdiag = jnp.asarray((blk[:, None] == blk[None, :]).astype(np.float32) / HEAD_DIM, BF16)
    lam_init = 0.8 - 0.6 * math.exp(-0.3 * l)
    lam = (jnp.exp(jnp.sum(w["lam_q1"][l] * w["lam_k1"][l]))
           - jnp.exp(jnp.sum(w["lam_q2"][l] * w["lam_k2"][l])) + lam_init)
    return dict(
        ffn1_g=w["ffn1_norm"][l][None], ffn1_wi=w["ffn1_wi"][l].astype(BF16), ffn1_wo=w["ffn1_wo"][l].astype(BF16),
        mix_g=w["mix_norm"][l][None], w_in=w["w_in"][l].astype(BF16),
        head_gains=head_gains, blockdiag=blockdiag, bias=_natten_bias(w["rpb"][l]),
        scalars=jnp.stack([lam, jnp.asarray(1.0 - lam_init, F32)]).astype(F32),
        subln=w["subln"][l][:, None],
        w_a=w["w_a_out"][l].astype(BF16), w_b=w["w_b_out"][l].astype(BF16), w_o=w["w_o"][l].astype(BF16),
        ffn2_g=w["ffn2_norm"][l][None], ffn2_wi=w["ffn2_wi"][l].astype(BF16), ffn2_wo=w["ffn2_wo"][l].astype(BF16),
    )


def _encoder_layer(x, c, rope, batch, seq_len):
    x = _ffn(x, c["ffn1_g"], c["ffn1_wi"], c["ffn1_wo"])
    proj, vt = _in_proj(x, c["mix_g"], c["w_in"], c["head_gains"], c["blockdiag"], *rope, batch, seq_len)
    ya = _natten(proj, c["bias"], batch, seq_len)
    yb = _diffattn(proj, vt, c["scalars"], c["subln"], batch, seq_len)
    x = _merge(x, ya, yb, proj, c["w_a"], c["w_b"], c["w_o"])
    return _ffn(x, c["ffn2_g"], c["ffn2_wi"], c["ffn2_wo"])


def _encoder(groups, weights, depth):
    consts = [_layer_consts(l, weights) for l in range(depth)]
    outs = []
    for x in groups:
        batch, seq_len, _ = x.shape
        rope = _rope_tables(seq_len)
        y = x.reshape(batch * seq_len, D_MODEL)
        for c in consts:
            y = _encoder_layer(y, c, rope, batch, seq_len)
        outs.append(y.reshape(x.shape))
    return tuple(outs)


def kernel(x_prompt, x_sample, ffn1_norm, ffn1_wi, ffn1_wo, mix_norm, w_in, qa_norm, ka_norm, rpb, qb_norm, kb_norm, lam_q1, lam_k1, lam_q2, lam_k2, subln, w_a_out, w_b_out, w_o, ffn2_norm, ffn2_wi, ffn2_wo):
    weights = dict(ffn1_norm=ffn1_norm, ffn1_wi=ffn1_wi, ffn1_wo=ffn1_wo, mix_norm=mix_norm, w_in=w_in,
                   qa_norm=qa_norm, ka_norm=ka_norm, rpb=rpb, qb_norm=qb_norm, kb_norm=kb_norm,
                   lam_q1=lam_q1, lam_k1=lam_k1, lam_q2=lam_q2, lam_k2=lam_k2, subln=subln,
                   w_a_out=w_a_out, w_b_out=w_b_out, w_o=w_o,
                   ffn2_norm=ffn2_norm, ffn2_wi=ffn2_wi, ffn2_wo=ffn2_wo)
    return _encoder((x_prompt, x_sample), weights, ffn1_norm.shape[0])
```

```python
import functools
import math

import jax
import jax.numpy as jnp
import numpy as np
from jax import lax
from jax.experimental import pallas as pl
from jax.experimental.pallas import tpu as pltpu

F32 = jnp.float32
BF16 = jnp.bfloat16

D_MODEL = 1024
GRID_W = 64
HEAD_DIM = 64
NA_HEADS = 8
DIFF_HEADS = 4
NA_WIDTH = NA_HEADS * HEAD_DIM
DIFF_V_DIM = 2 * HEAD_DIM
IN_WIDTH = 5120
D_FF = 2816
WIN_H = 8
WIN_W = 16
ROPE_THETA = 500000.0
ROPE_DIM = HEAD_DIM // 4
NORM_EPS = 1e-6
SUBLN_EPS = 1e-5
NEG_INF = -1e30
FFN_RES = 0.5

V7X_VMEM_BYTES = 64 * 1024 * 1024
V7X_LANES = 128
VMEM_LIMIT = V7X_VMEM_BYTES * 7 // 8

PROJ_BLOCK = 512
W_QA, W_KA, W_VA, W_QB, W_KB, W_VB, W_GA, W_GB = 0, 512, 1024, 1536, 2048, 2560, 3072, 4096
GA_OFF, GB_OFF, QA_OFF, KA_OFF, VA_OFF, QB_OFF, KB_OFF = 0, 1024, 2048, 2560, 3072, 3584, 4096
PROJ_OUT = 4608
DIFF_TK_CHOICES = (1024, 512, 256)
DIFF_MIN_CHUNKS = 8
BF16_SUBLANES = 16
DIFF_VT_ROWS = DIFF_V_DIM + BF16_SUBLANES
LOG2_E = math.log2(math.e)


def _params(*semantics):
    return pltpu.CompilerParams(dimension_semantics=semantics, vmem_limit_bytes=VMEM_LIMIT)


def _rms(x, g, eps):
    return x * lax.rsqrt(jnp.mean(x * x, axis=-1, keepdims=True) + eps) * g


def _ffn_kernel(x_ref, g_ref, wi_ref, wo_ref, o_ref, *, chunk):
    x = x_ref[...]
    h = _rms(x, g_ref[...], NORM_EPS).astype(BF16)
    n_chunks = D_FF // chunk

    def up(c):
        a = jnp.dot(h, wi_ref[:, c * chunk:(c + 1) * chunk], preferred_element_type=F32)
        b = jnp.dot(h, wi_ref[:, D_FF + c * chunk:D_FF + (c + 1) * chunk], preferred_element_type=F32)
        return a, b

    acc = jnp.zeros(x.shape, F32)
    nxt = up(0)
    for c in range(n_chunks):
        a, b = nxt
        if c + 1 < n_chunks:
            nxt = up(c + 1)
        gate = (a * jax.nn.sigmoid(a) * b).astype(BF16)
        acc = acc + jnp.dot(gate, wo_ref[c * chunk:(c + 1) * chunk, :], preferred_element_type=F32)
    o_ref[...] = x + FFN_RES * acc


def _ffn(x, g, wi, wo, *, tm=512, chunk=256):
    n = x.shape[0]
    return pl.pallas_call(
        functools.partial(_ffn_kernel, chunk=chunk),
        grid=(n // tm,),
        in_specs=[
            pl.BlockSpec((tm, D_MODEL), lambda i: (i, 0)),
            pl.BlockSpec((1, D_MODEL), lambda i: (0, 0)),
            pl.BlockSpec((D_MODEL, 2 * D_FF), lambda i: (0, 0)),
            pl.BlockSpec((D_FF, D_MODEL), lambda i: (0, 0)),
        ],
        out_specs=pl.BlockSpec((tm, D_MODEL), lambda i: (i, 0)),
        out_shape=jax.ShapeDtypeStruct((n, D_MODEL), F32),
        compiler_params=_params("parallel"),
        name="ffn",
    )(x, g, wi, wo)


def _proj_kernel(x_ref, g_ref, w_ref, hg_ref, bd_ref, cos_ref, sa_ref, sb_ref, o_ref, vt_ref):
    h = _rms(x_ref[...], g_ref[...], NORM_EPS).astype(BF16)
    tk = vt_ref.shape[-1]

    def proj(w_off):
        return jnp.dot(h, w_ref[:, w_off:w_off + PROJ_BLOCK], preferred_element_type=F32)

    def head_norm(p, gain_row):
        ms = jnp.dot((p * p).astype(BF16), bd_ref[...], preferred_element_type=F32)
        return p * lax.rsqrt(ms + NORM_EPS) * hg_ref[gain_row:gain_row + 1, :]

    reps = PROJ_BLOCK // V7X_LANES
    cos = jnp.concatenate([cos_ref[...]] * reps, axis=1)
    sa = jnp.concatenate([sa_ref[...]] * reps, axis=1)
    sb = jnp.concatenate([sb_ref[...]] * reps, axis=1)

    def rope(y):
        half = ROPE_DIM // 2
        up = pltpu.roll(y, PROJ_BLOCK - half, 1)
        dn = pltpu.roll(y, half, 1)
        return y * cos + up * sa + dn * sb

    def put(off, val):
        o_ref[:, off:off + PROJ_BLOCK] = val.astype(o_ref.dtype)

    put(QA_OFF, head_norm(proj(W_QA), 0))
    put(KA_OFF, head_norm(proj(W_KA), 1))
    put(VA_OFF, proj(W_VA))
    put(QB_OFF, rope(head_norm(proj(W_QB), 2)))
    put(KB_OFF, rope(head_norm(proj(W_KB), 3)))
    vb = proj(W_VB)
    pad_row = lax.broadcasted_iota(jnp.int32, (BF16_SUBLANES, tk), 0)
    ones_rows = jnp.where(pad_row == 0, 1.0, 0.0).astype(vt_ref.dtype)
    for head in range(DIFF_HEADS):
        for c in range(vt_ref.shape[1]):
            blk = vb[c * tk:(c + 1) * tk, head * DIFF_V_DIM:(head + 1) * DIFF_V_DIM]
            vt_ref[head, c, :DIFF_V_DIM, :] = blk.T.astype(vt_ref.dtype)
            vt_ref[head, c, DIFF_V_DIM:, :] = ones_rows
    for half in range(D_MODEL // PROJ_BLOCK):
        put(GA_OFF + half * PROJ_BLOCK, jax.nn.sigmoid(proj(W_GA + half * PROJ_BLOCK)))
        put(GB_OFF + half * PROJ_BLOCK, jax.nn.sigmoid(proj(W_GB + half * PROJ_BLOCK)))


def _diff_key_chunk(seq_len):
    fits = [t for t in DIFF_TK_CHOICES if seq_len % (2 * t) == 0]
    deep = [t for t in fits if seq_len // t >= DIFF_MIN_CHUNKS]
    return (deep or fits[-1:])[0]


def _in_proj(x, g, w, head_gains, blockdiag, cos, sa, sb, batch, seq_len, *, tm, tk):
    n = x.shape[0]
    tiles_per_seq = seq_len // tm
    return pl.pallas_call(
        _proj_kernel,
        grid=(n // tm,),
        in_specs=[
            pl.BlockSpec((tm, D_MODEL), lambda i: (i, 0)),
            pl.BlockSpec((1, D_MODEL), lambda i: (0, 0)),
            pl.BlockSpec((D_MODEL, IN_WIDTH), lambda i: (0, 0)),
            pl.BlockSpec((4, PROJ_BLOCK), lambda i: (0, 0)),
            pl.BlockSpec((PROJ_BLOCK, PROJ_BLOCK), lambda i: (0, 0)),
            pl.BlockSpec((tm, V7X_LANES), lambda i: (i % tiles_per_seq, 0)),
            pl.BlockSpec((tm, V7X_LANES), lambda i: (i % tiles_per_seq, 0)),
            pl.BlockSpec((tm, V7X_LANES), lambda i: (i % tiles_per_seq, 0)),
        ],
        out_specs=[
            pl.BlockSpec((tm, PROJ_OUT), lambda i: (i, 0)),
            pl.BlockSpec((None, DIFF_HEADS, tm // tk, DIFF_VT_ROWS, tk),
                         lambda i: (i // tiles_per_seq, 0, i % tiles_per_seq, 0, 0)),
        ],
        out_shape=[
            jax.ShapeDtypeStruct((n, PROJ_OUT), BF16),
            jax.ShapeDtypeStruct((batch, DIFF_HEADS, seq_len // tk, DIFF_VT_ROWS, tk), BF16),
        ],
        compiler_params=_params("parallel"),
        name="in_proj",
    )(x, g, w, head_gains, blockdiag, cos, sa, sb)


def _natten_kernel(q_ref, k_ref, v_ref, bias_ref, o_ref, *, rows, rows_per_step):
    n_keys = WIN_H * GRID_W
    r0 = pl.program_id(1) * rows_per_step
    block_start = jnp.clip(r0 - WIN_H // 2, 0, rows - (rows_per_step + WIN_H - 1))
    lane = lax.broadcasted_iota(jnp.int32, (GRID_W, V7X_LANES), 1)
    lo = lane < HEAD_DIM
    tasks = [(i, pair) for i in range(rows_per_step) for pair in range(NA_HEADS // 2)]

    def key_rows(i):
        start = jnp.clip(r0 + i - WIN_H // 2, 0, rows - WIN_H)
        variant = start - (r0 + i) + WIN_H - 1
        return pl.ds(pl.multiple_of((start - block_start) * GRID_W, GRID_W), n_keys), variant

    def scores(i, pair):
        keys, variant = key_rows(i)
        cols = slice(pair * V7X_LANES, (pair + 1) * V7X_LANES)
        qp = q_ref[i * GRID_W:(i + 1) * GRID_W, cols]
        zero = jnp.zeros_like(qp)
        lhs = jnp.concatenate([jnp.where(lo, qp, zero), jnp.where(lo, zero, qp)], axis=0)
        s = lax.dot_general(lhs, k_ref[keys, cols], (((1,), (1,)), ((), ())), preferred_element_type=F32)
        return s + bias_ref[variant, pair * 2 * GRID_W:(pair + 1) * 2 * GRID_W, :]

    s_next = scores(*tasks[0])
    for n, (i, pair) in enumerate(tasks):
        s = s_next
        if n + 1 < len(tasks):
            s_next = scores(*tasks[n + 1])
        keys, _ = key_rows(i)
        cols = slice(pair * V7X_LANES, (pair + 1) * V7X_LANES)
        m = jnp.max(s, axis=-1, keepdims=True)
        e = jnp.exp(s - m)
        l = jnp.sum(e, axis=-1, keepdims=True)
        o = jnp.dot(e.astype(BF16), v_ref[keys, cols], preferred_element_type=F32)
        o = o * (1.0 / l)
        o_ref[i * GRID_W:(i + 1) * GRID_W, cols] = jnp.where(lo, o[:GRID_W], o[GRID_W:]).astype(o_ref.dtype)


def _natten(proj, bias, batch, seq_len, *, rows_per_step=4):
    rows = seq_len // GRID_W
    n = proj.shape[0]
    block_rows = rows_per_step + WIN_H - 1
    assert rows % rows_per_step == 0 and rows >= block_rows
    steps = rows // rows_per_step

    def block_start(j):
        return jnp.clip(j * rows_per_step - WIN_H // 2, 0, rows - block_rows)

    return pl.pallas_call(
        functools.partial(_natten_kernel, rows=rows, rows_per_step=rows_per_step),
        grid=(batch, steps),
        in_specs=[
            pl.BlockSpec((rows_per_step * GRID_W, NA_WIDTH), lambda b, j: (b * steps + j, QA_OFF // NA_WIDTH)),
            pl.BlockSpec((pl.Element(block_rows * GRID_W), pl.Element(NA_WIDTH)),
                         lambda b, j: ((b * rows + block_start(j)) * GRID_W, KA_OFF)),
            pl.BlockSpec((pl.Element(block_rows * GRID_W), pl.Element(NA_WIDTH)),
                         lambda b, j: ((b * rows + block_start(j)) * GRID_W, VA_OFF)),
            pl.BlockSpec((WIN_H, NA_HEADS * GRID_W, WIN_H * GRID_W), lambda b, j: (0, 0, 0)),
        ],
        out_specs=pl.BlockSpec((rows_per_step * GRID_W, NA_WIDTH), lambda b, j: (b * steps + j, 0)),
        out_shape=jax.ShapeDtypeStruct((n, NA_WIDTH), BF16),
        compiler_params=_params("parallel", "arbitrary"),
        name="natten",
    )(proj, proj, proj, bias)


def _natten_bias(rpb):
    qc = np.arange(GRID_W)[:, None]
    kc = np.arange(GRID_W)[None, :]
    wstart = np.clip(qc - WIN_W // 2, 0, GRID_W - WIN_W)
    in_win = (kc >= wstart) & (kc < wstart + WIN_W)
    dc = np.clip(kc - qc + WIN_W - 1, 0, 2 * WIN_W - 2)
    dr = np.arange(WIN_H)[:, None] + np.arange(WIN_H)[None, :]
    b = rpb.astype(F32)[:, dr][:, :, :, dc]
    b = jnp.where(in_win[None, None, None], b, NEG_INF)
    b = b.transpose(1, 0, 3, 2, 4)
    return b.reshape(WIN_H, NA_HEADS * GRID_W, WIN_H * GRID_W)


def _diff_kernel(sc_ref, q_ref, k_ref, vt_ref, g_ref, o_ref,
                 qst_scr, s_a, s_b, cm_a, cm_b, p_a, p_b, al_a, al_b, m_scr, acc_scr, *, tk, heads):
    tq = q_ref.shape[0]
    n_chunks = k_ref.shape[0] // tk
    bufs = ((s_a, cm_a, p_a, al_a), (s_b, cm_b, p_b, al_b))

    def head_cols(h):
        return slice(h * DIFF_V_DIM, (h + 1) * DIFF_V_DIM)

    def prepare(h):
        qt = q_ref[:, head_cols(h)].astype(F32).T
        row = lax.broadcasted_iota(jnp.int32, qt.shape, 0)
        zero = jnp.zeros_like(qt)
        qst_scr[h] = jnp.concatenate(
            [jnp.where(row < HEAD_DIM, qt, zero), jnp.where(row < HEAD_DIM, zero, qt)], axis=1).astype(BF16)
        m_scr[h] = jnp.full(m_scr.shape[1:], NEG_INF, F32)
        acc_scr[h] = jnp.zeros(acc_scr.shape[1:], F32)

    def scores(h, c, buf):
        s_ref, cm_ref, _, _ = buf
        start = pl.multiple_of(c * tk, tk)
        s = jnp.dot(k_ref[pl.ds(start, tk), head_cols(h)], qst_scr[h], preferred_element_type=F32)
        s_ref[...] = s
        cm_ref[...] = jnp.max(s, axis=0, keepdims=True)

    def softmax(h, buf):
        s_ref, cm_ref, p_ref, al_ref = buf
        m_prev = m_scr[h]
        m_new = jnp.maximum(m_prev, cm_ref[...])
        al_ref[...] = jnp.exp2(m_prev - m_new)
        m_scr[h] = m_new
        p_ref[...] = jnp.exp2(s_ref[...] - m_new).astype(BF16)

    def accumulate(h, c, buf):
        _, _, p_ref, al_ref = buf
        acc_scr[h] = al_ref[...] * acc_scr[h] + jnp.dot(vt_ref[h, c], p_ref[...], preferred_element_type=F32)

    def finalize(h):
        acc = acc_scr[h]
        o = acc[:DIFF_V_DIM] * (1.0 / acc[DIFF_V_DIM:DIFF_V_DIM + 1])
        d = o[:, :tq] - sc_ref[0] * o[:, tq:]
        ms = jnp.mean(d * d, axis=0, keepdims=True)
        y = d * lax.rsqrt(ms + SUBLN_EPS) * (g_ref[...] * sc_ref[1])
        o_ref[:, head_cols(h)] = y.T.astype(o_ref.dtype)

    def turn(h, c, parity, nxt, prev):
        if nxt is not None:
            if nxt[0] != h:
                prepare(nxt[0])
            scores(nxt[0], nxt[1], bufs[1 - parity])
        softmax(h, bufs[parity])
        if prev is not None:
            accumulate(prev[0], prev[1], bufs[1 - parity])
            if prev[0] != h:
                finalize(prev[0])

    prepare(0)
    scores(0, 0, bufs[0])
    if heads == 1 and n_chunks > 4 and (n_chunks - 4) % 4 == 0:
        turn(0, 0, 0, (0, 1), None)
        turn(0, 1, 1, (0, 2), (0, 0))

        def body(i, carry):
            for j in range(4):
                c = 2 + 4 * i + j
                turn(0, c, j % 2, (0, c + 1), (0, c - 1))
            return carry
        lax.fori_loop(0, (n_chunks - 4) // 4, body, 0)
        turn(0, n_chunks - 2, 0, (0, n_chunks - 1), (0, n_chunks - 3))
        turn(0, n_chunks - 1, 1, None, (0, n_chunks - 2))
    else:
        tasks = [(h, c) for h in range(heads) for c in range(n_chunks)]
        for t, (h, c) in enumerate(tasks):
            turn(h, c, t % 2, tasks[t + 1] if t + 1 < len(tasks) else None, tasks[t - 1] if t else None)
    accumulate(heads - 1, n_chunks - 1, bufs[1])
    finalize(heads - 1)


def _diffattn(proj, vt, scalars, subln_col, batch, seq_len, *, tq=512, max_unrolled_tasks=32):
    n = proj.shape[0]
    tk = vt.shape[-1]
    n_chunks = seq_len // tk
    q_tiles = seq_len // tq
    assert n_chunks % 2 == 0
    heads = DIFF_HEADS if DIFF_HEADS * n_chunks <= max_unrolled_tasks else 1
    width = heads * DIFF_V_DIM
    return pl.pallas_call(
        functools.partial(_diff_kernel, tk=tk, heads=heads),
        grid=(batch, DIFF_HEADS // heads, q_tiles),
        in_specs=[
            pl.BlockSpec(memory_space=pltpu.SMEM),
            pl.BlockSpec((tq, width), lambda b, h, i: (b * q_tiles + i, QB_OFF // width + h)),
            pl.BlockSpec((seq_len, width), lambda b, h, i: (b, KB_OFF // width + h)),
            pl.BlockSpec((None, heads, n_chunks, DIFF_VT_ROWS, tk), lambda b, h, i: (b, h, 0, 0, 0)),
            pl.BlockSpec((DIFF_V_DIM, 1), lambda b, h, i: (0, 0)),
        ],
        out_specs=pl.BlockSpec((tq, width), lambda b, h, i: (b * q_tiles + i, h)),
        out_shape=jax.ShapeDtypeStruct((n, DIFF_HEADS * DIFF_V_DIM), BF16),
        scratch_shapes=[
            pltpu.VMEM((heads, DIFF_V_DIM, 2 * tq), BF16),
            pltpu.VMEM((tk, 2 * tq), F32),
            pltpu.VMEM((tk, 2 * tq), F32),
            pltpu.VMEM((1, 2 * tq), F32),
            pltpu.VMEM((1, 2 * tq), F32),
            pltpu.VMEM((tk, 2 * tq), BF16),
            pltpu.VMEM((tk, 2 * tq), BF16),
            pltpu.VMEM((1, 2 * tq), F32),
            pltpu.VMEM((1, 2 * tq), F32),
            pltpu.VMEM((heads, 1, 2 * tq), F32),
            pltpu.VMEM((heads, DIFF_VT_ROWS, 2 * tq), F32),
        ],
        compiler_params=_params("parallel", "parallel", "arbitrary"),
        name="diffattn",
    )(scalars, proj, proj, vt, subln_col)


def _merge_kernel(x_ref, ya_ref, yb_ref, ga_ref, gb_ref, wa_ref, wb_ref, wo_ref, o_ref):
    a = jnp.dot(ya_ref[...], wa_ref[...], preferred_element_type=F32)
    b = jnp.dot(yb_ref[...], wb_ref[...], preferred_element_type=F32)
    merged = ga_ref[...].astype(F32) * a + gb_ref[...].astype(F32) * b
    o_ref[...] = x_ref[...] + jnp.dot(merged.astype(BF16), wo_ref[...], preferred_element_type=F32)


def _merge(x, ya, yb, proj, wa, wb, wo, *, tm=512):
    n = x.shape[0]
    return pl.pallas_call(
        _merge_kernel,
        grid=(n // tm,),
        in_specs=[
            pl.BlockSpec((tm, D_MODEL), lambda i: (i, 0)),
            pl.BlockSpec((tm, NA_WIDTH), lambda i: (i, 0)),
            pl.BlockSpec((tm, NA_WIDTH), lambda i: (i, 0)),
            pl.BlockSpec((tm, D_MODEL), lambda i: (i, GA_OFF // D_MODEL)),
            pl.BlockSpec((tm, D_MODEL), lambda i: (i, GB_OFF // D_MODEL)),
            pl.BlockSpec((NA_WIDTH, D_MODEL), lambda i: (0, 0)),
            pl.BlockSpec((NA_WIDTH, D_MODEL), lambda i: (0, 0)),
            pl.BlockSpec((D_MODEL, D_MODEL), lambda i: (0, 0)),
        ],
        out_specs=pl.BlockSpec((tm, D_MODEL), lambda i: (i, 0)),
        out_shape=jax.ShapeDtypeStruct((n, D_MODEL), F32),
        compiler_params=_params("parallel"),
        name="merge",
    )(x, ya, yb, proj, proj, wa, wb, wo)


def _rope_tables(seq_len):
    half = ROPE_DIM // 2
    inv_freq = jnp.power(ROPE_THETA, -jnp.arange(half, dtype=F32) * 2.0 / ROPE_DIM)
    ang = jnp.arange(seq_len, dtype=F32)[:, None] * inv_freq[None, :]
    cos, sin = jnp.cos(ang), jnp.sin(ang)
    ones = jnp.ones((seq_len, HEAD_DIM - ROPE_DIM), F32)
    zeros = jnp.zeros((seq_len, HEAD_DIM - ROPE_DIM), F32)
    zh = jnp.zeros_like(sin)
    c = jnp.concatenate([cos, cos, ones], axis=1)
    sa = jnp.concatenate([-sin, zh, zeros], axis=1)
    sb = jnp.concatenate([zh, sin, zeros], axis=1)
    reps = V7X_LANES // HEAD_DIM
    return tuple(jnp.tile(t, (1, reps)) for t in (c, sa, sb))


def _layer_consts(l, w):
    scale = HEAD_DIM ** -0.5
    heads = PROJ_BLOCK // HEAD_DIM
    head_gains = jnp.stack([
        jnp.tile(w["qa_norm"][l] * scale, heads), jnp.tile(w["ka_norm"][l], heads),
        jnp.tile(w["qb_norm"][l] * (scale * LOG2_E), heads), jnp.tile(w["kb_norm"][l], heads)])
    blk = np.arange(PROJ_BLOCK) // HEAD_DIM
    blockdiag = jnp.asarray((blk[:, None] == blk[None, :]).astype(np.float32) / HEAD_DIM, BF16)
    lam_init = 0.8 - 0.6 * math.exp(-0.3 * l)
    lam = (jnp.exp(jnp.sum(w["lam_q1"][l] * w["lam_k1"][l]))
           - jnp.exp(jnp.sum(w["lam_q2"][l] * w["lam_k2"][l])) + lam_init)
    return dict(
        ffn1_g=w["ffn1_norm"][l][None], ffn1_wi=w["ffn1_wi"][l].astype(BF16), ffn1_wo=w["ffn1_wo"][l].astype(BF16),
        mix_g=w["mix_norm"][l][None], w_in=w["w_in"][l].astype(BF16),
        head_gains=head_gains, blockdiag=blockdiag, bias=_natten_bias(w["rpb"][l]),
        scalars=jnp.stack([lam, jnp.asarray(1.0 - lam_init, F32)]).astype(F32),
        subln=w["subln"][l][:, None],
        w_a=w["w_a_out"][l].astype(BF16), w_b=w["w_b_out"][l].astype(BF16), w_o=w["w_o"][l].astype(BF16),
        ffn2_g=w["ffn2_norm"][l][None], ffn2_wi=w["ffn2_wi"][l].astype(BF16), ffn2_wo=w["ffn2_wo"][l].astype(BF16),
    )


def _encoder_layer(x, c, rope, batch, seq_len):
    x = _ffn(x, c["ffn1_g"], c["ffn1_wi"], c["ffn1_wo"])
    tk = _diff_key_chunk(seq_len)
    proj, vt = _in_proj(x, c["mix_g"], c["w_in"], c["head_gains"], c["blockdiag"], *rope, batch, seq_len,
                        tm=max(tk, PROJ_BLOCK), tk=tk)
    ya = _natten(proj, c["bias"], batch, seq_len)
    yb = _diffattn(proj, vt, c["scalars"], c["subln"], batch, seq_len)
    x = _merge(x, ya, yb, proj, c["w_a"], c["w_b"], c["w_o"])
    return _ffn(x, c["ffn2_g"], c["ffn2_wi"], c["ffn2_wo"])


def _encoder(groups, weights, depth):
    consts = [_layer_consts(l, weights) for l in range(depth)]
    outs = []
    for x in groups:
        batch, seq_len, _ = x.shape
        rope = _rope_tables(seq_len)
        y = x.reshape(batch * seq_len, D_MODEL)
        for c in consts:
            y = _encoder_layer(y, c, rope, batch, seq_len)
        outs.append(y.reshape(x.shape))
    return tuple(outs)


def kernel(x_prompt, x_sample, ffn1_norm, ffn1_wi, ffn1_wo, mix_norm, w_in, qa_norm, ka_norm, rpb, qb_norm, kb_norm, lam_q1, lam_k1, lam_q2, lam_k2, subln, w_a_out, w_b_out, w_o, ffn2_norm, ffn2_wi, ffn2_wo):
    weights = dict(ffn1_norm=ffn1_norm, ffn1_wi=ffn1_wi, ffn1_wo=ffn1_wo, mix_norm=mix_norm, w_in=w_in,
                   qa_norm=qa_norm, ka_norm=ka_norm, rpb=rpb, qb_norm=qb_norm, kb_norm=kb_norm,
                   lam_q1=lam_q1, lam_k1=lam_k1, lam_q2=lam_q2, lam_k2=lam_k2, subln=subln,
                   w_a_out=w_a_out, w_b_out=w_b_out, w_o=w_o,
                   ffn2_norm=ffn2_norm, ffn2_wi=ffn2_wi, ffn2_wo=ffn2_wo)
    return _encoder((x_prompt, x_sample), weights, ffn1_norm.shape[0])
```

```python
import functools
import math

import jax
import jax.numpy as jnp
import numpy as np
from jax import lax
from jax.experimental import pallas as pl
from jax.experimental.pallas import tpu as pltpu

F32 = jnp.float32
BF16 = jnp.bfloat16

D_MODEL = 1024
GRID_W = 64
HEAD_DIM = 64
NA_HEADS = 8
DIFF_HEADS = 4
NA_WIDTH = NA_HEADS * HEAD_DIM
DIFF_V_DIM = 2 * HEAD_DIM
IN_WIDTH = 5120
D_FF = 2816
WIN_H = 8
WIN_W = 16
ROPE_THETA = 500000.0
ROPE_DIM = HEAD_DIM // 4
NORM_EPS = 1e-6
SUBLN_EPS = 1e-5
NEG_INF = -1e30
FFN_RES = 0.5

V7X_VMEM_BYTES = 64 * 1024 * 1024
V7X_LANES = 128
VMEM_LIMIT = V7X_VMEM_BYTES * 7 // 8

PROJ_BLOCK = 512
W_QA, W_KA, W_VA, W_QB, W_KB, W_VB, W_GA, W_GB = 0, 512, 1024, 1536, 2048, 2560, 3072, 4096
GA_OFF, GB_OFF, QA_OFF, KA_OFF, VA_OFF, QB_OFF, KB_OFF = 0, 1024, 2048, 2560, 3072, 3584, 4096
PROJ_OUT = 4608
DIFF_TK_CHOICES = (1024, 512, 256)
DIFF_MIN_CHUNKS = 8
BF16_SUBLANES = 16
DIFF_VT_ROWS = DIFF_V_DIM + BF16_SUBLANES
LOG2_E = math.log2(math.e)
NORM_SLACK = 1.05
DIFF_MAX_UNSTABILIZED_SCORE = 32.0


def _params(*semantics):
    return pltpu.CompilerParams(dimension_semantics=semantics, vmem_limit_bytes=VMEM_LIMIT)


def _rms(x, g, eps):
    return x * lax.rsqrt(jnp.mean(x * x, axis=-1, keepdims=True) + eps) * g


def _ffn_kernel(x_ref, g_ref, wi_ref, wo_ref, o_ref, *, chunk):
    x = x_ref[...]
    h = _rms(x, g_ref[...], NORM_EPS).astype(BF16)
    n_chunks = D_FF // chunk

    def up(c):
        a = jnp.dot(h, wi_ref[:, c * chunk:(c + 1) * chunk], preferred_element_type=F32)
        b = jnp.dot(h, wi_ref[:, D_FF + c * chunk:D_FF + (c + 1) * chunk], preferred_element_type=F32)
        return a, b

    acc = jnp.zeros(x.shape, F32)
    nxt = up(0)
    for c in range(n_chunks):
        a, b = nxt
        if c + 1 < n_chunks:
            nxt = up(c + 1)
        gate = (a * jax.nn.sigmoid(a) * b).astype(BF16)
        acc = acc + jnp.dot(gate, wo_ref[c * chunk:(c + 1) * chunk, :], preferred_element_type=F32)
    o_ref[...] = x + FFN_RES * acc


def _ffn(x, g, wi, wo, *, tm=512, chunk=256):
    n = x.shape[0]
    return pl.pallas_call(
        functools.partial(_ffn_kernel, chunk=chunk),
        grid=(n // tm,),
        in_specs=[
            pl.BlockSpec((tm, D_MODEL), lambda i: (i, 0)),
            pl.BlockSpec((1, D_MODEL), lambda i: (0, 0)),
            pl.BlockSpec((D_MODEL, 2 * D_FF), lambda i: (0, 0)),
            pl.BlockSpec((D_FF, D_MODEL), lambda i: (0, 0)),
        ],
        out_specs=pl.BlockSpec((tm, D_MODEL), lambda i: (i, 0)),
        out_shape=jax.ShapeDtypeStruct((n, D_MODEL), F32),
        compiler_params=_params("parallel"),
        name="ffn",
    )(x, g, wi, wo)


def _proj_kernel(x_ref, g_ref, w_ref, hg_ref, bd_ref, cos_ref, sa_ref, sb_ref, o_ref, vt_ref):
    h = _rms(x_ref[...], g_ref[...], NORM_EPS).astype(BF16)
    tk = vt_ref.shape[-1]

    def proj(w_off):
        return jnp.dot(h, w_ref[:, w_off:w_off + PROJ_BLOCK], preferred_element_type=F32)

    def head_norm(p, gain_row):
        ms = jnp.dot((p * p).astype(BF16), bd_ref[...], preferred_element_type=F32)
        return p * lax.rsqrt(ms + NORM_EPS) * hg_ref[gain_row:gain_row + 1, :]

    reps = PROJ_BLOCK // V7X_LANES
    cos = jnp.concatenate([cos_ref[...]] * reps, axis=1)
    sa = jnp.concatenate([sa_ref[...]] * reps, axis=1)
    sb = jnp.concatenate([sb_ref[...]] * reps, axis=1)

    def rope(y):
        half = ROPE_DIM // 2
        up = pltpu.roll(y, PROJ_BLOCK - half, 1)
        dn = pltpu.roll(y, half, 1)
        return y * cos + up * sa + dn * sb

    def put(off, val):
        o_ref[:, off:off + PROJ_BLOCK] = val.astype(o_ref.dtype)

    put(QA_OFF, head_norm(proj(W_QA), 0))
    put(KA_OFF, head_norm(proj(W_KA), 1))
    put(VA_OFF, proj(W_VA))
    put(QB_OFF, rope(head_norm(proj(W_QB), 2)))
    put(KB_OFF, rope(head_norm(proj(W_KB), 3)))
    vb = proj(W_VB)
    pad_row = lax.broadcasted_iota(jnp.int32, (BF16_SUBLANES, tk), 0)
    ones_rows = jnp.where(pad_row == 0, 1.0, 0.0).astype(vt_ref.dtype)
    for head in range(DIFF_HEADS):
        for c in range(vt_ref.shape[1]):
            blk = vb[c * tk:(c + 1) * tk, head * DIFF_V_DIM:(head + 1) * DIFF_V_DIM]
            vt_ref[head, c, :DIFF_V_DIM, :] = blk.T.astype(vt_ref.dtype)
            vt_ref[head, c, DIFF_V_DIM:, :] = ones_rows
    for half in range(D_MODEL // PROJ_BLOCK):
        put(GA_OFF + half * PROJ_BLOCK, jax.nn.sigmoid(proj(W_GA + half * PROJ_BLOCK)))
        put(GB_OFF + half * PROJ_BLOCK, jax.nn.sigmoid(proj(W_GB + half * PROJ_BLOCK)))


def _diff_key_chunk(seq_len):
    fits = [t for t in DIFF_TK_CHOICES if seq_len % (2 * t) == 0]
    deep = [t for t in fits if seq_len // t >= DIFF_MIN_CHUNKS]
    return (deep or fits[-1:])[0]


def _in_proj(x, g, w, head_gains, blockdiag, cos, sa, sb, batch, seq_len, *, tm, tk):
    n = x.shape[0]
    tiles_per_seq = seq_len // tm
    return pl.pallas_call(
        _proj_kernel,
        grid=(n // tm,),
        in_specs=[
            pl.BlockSpec((tm, D_MODEL), lambda i: (i, 0)),
            pl.BlockSpec((1, D_MODEL), lambda i: (0, 0)),
            pl.BlockSpec((D_MODEL, IN_WIDTH), lambda i: (0, 0)),
            pl.BlockSpec((4, PROJ_BLOCK), lambda i: (0, 0)),
            pl.BlockSpec((PROJ_BLOCK, PROJ_BLOCK), lambda i: (0, 0)),
            pl.BlockSpec((tm, V7X_LANES), lambda i: (i % tiles_per_seq, 0)),
            pl.BlockSpec((tm, V7X_LANES), lambda i: (i % tiles_per_seq, 0)),
            pl.BlockSpec((tm, V7X_LANES), lambda i: (i % tiles_per_seq, 0)),
        ],
        out_specs=[
            pl.BlockSpec((tm, PROJ_OUT), lambda i: (i, 0)),
            pl.BlockSpec((None, DIFF_HEADS, tm // tk, DIFF_VT_ROWS, tk),
                         lambda i: (i // tiles_per_seq, 0, i % tiles_per_seq, 0, 0)),
        ],
        out_shape=[
            jax.ShapeDtypeStruct((n, PROJ_OUT), BF16),
            jax.ShapeDtypeStruct((batch, DIFF_HEADS, seq_len // tk, DIFF_VT_ROWS, tk), BF16),
        ],
        compiler_params=_params("parallel"),
        name="in_proj",
    )(x, g, w, head_gains, blockdiag, cos, sa, sb)


def _natten_kernel(q_ref, k_ref, v_ref, bias_ref, o_ref, *, rows, rows_per_step):
    n_keys = WIN_H * GRID_W
    r0 = pl.program_id(1) * rows_per_step
    block_start = jnp.clip(r0 - WIN_H // 2, 0, rows - (rows_per_step + WIN_H - 1))
    lane = lax.broadcasted_iota(jnp.int32, (GRID_W, V7X_LANES), 1)
    lo = lane < HEAD_DIM
    tasks = [(i, pair) for i in range(rows_per_step) for pair in range(NA_HEADS // 2)]

    def key_rows(i):
        start = jnp.clip(r0 + i - WIN_H // 2, 0, rows - WIN_H)
        variant = start - (r0 + i) + WIN_H - 1
        return pl.ds(pl.multiple_of((start - block_start) * GRID_W, GRID_W), n_keys), variant

    def scores(i, pair):
        keys, variant = key_rows(i)
        cols = slice(pair * V7X_LANES, (pair + 1) * V7X_LANES)
        qp = q_ref[i * GRID_W:(i + 1) * GRID_W, cols]
        zero = jnp.zeros_like(qp)
        lhs = jnp.concatenate([jnp.where(lo, qp, zero), jnp.where(lo, zero, qp)], axis=0)
        s = lax.dot_general(lhs, k_ref[keys, cols], (((1,), (1,)), ((), ())), preferred_element_type=F32)
        return s + bias_ref[variant, pair * 2 * GRID_W:(pair + 1) * 2 * GRID_W, :]

    s_next = scores(*tasks[0])
    for n, (i, pair) in enumerate(tasks):
        s = s_next
        if n + 1 < len(tasks):
            s_next = scores(*tasks[n + 1])
        keys, _ = key_rows(i)
        cols = slice(pair * V7X_LANES, (pair + 1) * V7X_LANES)
        m = jnp.max(s, axis=-1, keepdims=True)
        e = jnp.exp(s - m)
        l = jnp.sum(e, axis=-1, keepdims=True)
        o = jnp.dot(e.astype(BF16), v_ref[keys, cols], preferred_element_type=F32)
        o = o * (1.0 / l)
        o_ref[i * GRID_W:(i + 1) * GRID_W, cols] = jnp.where(lo, o[:GRID_W], o[GRID_W:]).astype(o_ref.dtype)


def _natten(proj, bias, batch, seq_len, *, rows_per_step=4):
    rows = seq_len // GRID_W
    n = proj.shape[0]
    block_rows = rows_per_step + WIN_H - 1
    assert rows % rows_per_step == 0 and rows >= block_rows
    steps = rows // rows_per_step

    def block_start(j):
        return jnp.clip(j * rows_per_step - WIN_H // 2, 0, rows - block_rows)

    return pl.pallas_call(
        functools.partial(_natten_kernel, rows=rows, rows_per_step=rows_per_step),
        grid=(batch, steps),
        in_specs=[
            pl.BlockSpec((rows_per_step * GRID_W, NA_WIDTH), lambda b, j: (b * steps + j, QA_OFF // NA_WIDTH)),
            pl.BlockSpec((pl.Element(block_rows * GRID_W), pl.Element(NA_WIDTH)),
                         lambda b, j: ((b * rows + block_start(j)) * GRID_W, KA_OFF)),
            pl.BlockSpec((pl.Element(block_rows * GRID_W), pl.Element(NA_WIDTH)),
                         lambda b, j: ((b * rows + block_start(j)) * GRID_W, VA_OFF)),
            pl.BlockSpec((WIN_H, NA_HEADS * GRID_W, WIN_H * GRID_W), lambda b, j: (0, 0, 0)),
        ],
        out_specs=pl.BlockSpec((rows_per_step * GRID_W, NA_WIDTH), lambda b, j: (b * steps + j, 0)),
        out_shape=jax.ShapeDtypeStruct((n, NA_WIDTH), BF16),
        compiler_params=_params("parallel", "arbitrary"),
        name="natten",
    )(proj, proj, proj, bias)


def _natten_bias(rpb):
    qc = np.arange(GRID_W)[:, None]
    kc = np.arange(GRID_W)[None, :]
    wstart = np.clip(qc - WIN_W // 2, 0, GRID_W - WIN_W)
    in_win = (kc >= wstart) & (kc < wstart + WIN_W)
    dc = np.clip(kc - qc + WIN_W - 1, 0, 2 * WIN_W - 2)
    dr = np.arange(WIN_H)[:, None] + np.arange(WIN_H)[None, :]
    b = rpb.astype(F32)[:, dr][:, :, :, dc]
    b = jnp.where(in_win[None, None, None], b, NEG_INF)
    b = b.transpose(1, 0, 3, 2, 4)
    return b.reshape(WIN_H, NA_HEADS * GRID_W, WIN_H * GRID_W)


def _diff_kernel(sc_ref, q_ref, k_ref, vt_ref, g_ref, o_ref,
                 qst_scr, s_a, s_b, cm_a, cm_b, p_a, p_b, al_a, al_b, m_scr, acc_scr, *, tk, heads, stabilize):
    tq = q_ref.shape[0]
    n_chunks = k_ref.shape[0] // tk
    bufs = ((s_a, cm_a, p_a, al_a), (s_b, cm_b, p_b, al_b))

    def head_cols(h):
        return slice(h * DIFF_V_DIM, (h + 1) * DIFF_V_DIM)

    def prepare(h):
        qt = q_ref[:, head_cols(h)].astype(F32).T
        row = lax.broadcasted_iota(jnp.int32, qt.shape, 0)
        zero = jnp.zeros_like(qt)
        qst_scr[h] = jnp.concatenate(
            [jnp.where(row < HEAD_DIM, qt, zero), jnp.where(row < HEAD_DIM, zero, qt)], axis=1).astype(BF16)
        m_scr[h] = jnp.full(m_scr.shape[1:], NEG_INF, F32)
        acc_scr[h] = jnp.zeros(acc_scr.shape[1:], F32)

    def scores(h, c, buf):
        s_ref, cm_ref, p_ref, _ = buf
        start = pl.multiple_of(c * tk, tk)
        s = jnp.dot(k_ref[pl.ds(start, tk), head_cols(h)], qst_scr[h], preferred_element_type=F32)
        if stabilize:
            s_ref[...] = s
            cm_ref[...] = jnp.max(s, axis=0, keepdims=True)
        else:
            p_ref[...] = jnp.exp2(s).astype(BF16)

    def softmax(h, buf):
        s_ref, cm_ref, p_ref, al_ref = buf
        m_prev = m_scr[h]
        m_new = jnp.maximum(m_prev, cm_ref[...])
        al_ref[...] = jnp.exp2(m_prev - m_new)
        m_scr[h] = m_new
        p_ref[...] = jnp.exp2(s_ref[...] - m_new).astype(BF16)

    def accumulate(h, c, buf):
        _, _, p_ref, al_ref = buf
        pv = jnp.dot(vt_ref[h, c], p_ref[...], preferred_element_type=F32)
        acc_scr[h] = (al_ref[...] * acc_scr[h] if stabilize else acc_scr[h]) + pv

    def finalize(h):
        acc = acc_scr[h]
        o = acc[:DIFF_V_DIM] * (1.0 / acc[DIFF_V_DIM:DIFF_V_DIM + 1])
        d = o[:, :tq] - sc_ref[0] * o[:, tq:]
        ms = jnp.mean(d * d, axis=0, keepdims=True)
        y = d * lax.rsqrt(ms + SUBLN_EPS) * (g_ref[...] * sc_ref[1])
        o_ref[:, head_cols(h)] = y.T.astype(o_ref.dtype)

    def turn(h, c, parity, nxt, prev):
        if nxt is not None:
            if nxt[0] != h:
                prepare(nxt[0])
            scores(nxt[0], nxt[1], bufs[1 - parity])
        if not stabilize:
            accumulate(h, c, bufs[parity])
            if nxt is None or nxt[0] != h:
                finalize(h)
            return
        softmax(h, bufs[parity])
        if prev is not None:
            accumulate(prev[0], prev[1], bufs[1 - parity])
            if prev[0] != h:
                finalize(prev[0])

    prepare(0)
    scores(0, 0, bufs[0])
    if heads == 1 and n_chunks > 4 and (n_chunks - 4) % 4 == 0:
        turn(0, 0, 0, (0, 1), None)
        turn(0, 1, 1, (0, 2), (0, 0))

        def body(i, carry):
            for j in range(4):
                c = 2 + 4 * i + j
                turn(0, c, j % 2, (0, c + 1), (0, c - 1))
            return carry
        lax.fori_loop(0, (n_chunks - 4) // 4, body, 0)
        turn(0, n_chunks - 2, 0, (0, n_chunks - 1), (0, n_chunks - 3))
        turn(0, n_chunks - 1, 1, None, (0, n_chunks - 2))
    else:
        tasks = [(h, c) for h in range(heads) for c in range(n_chunks)]
        for t, (h, c) in enumerate(tasks):
            turn(h, c, t % 2, tasks[t + 1] if t + 1 < len(tasks) else None, tasks[t - 1] if t else None)
    if stabilize:
        accumulate(heads - 1, n_chunks - 1, bufs[1])
        finalize(heads - 1)


def _diffattn(proj, vt, scalars, subln_col, batch, seq_len, *, stabilize, tq=512, max_unrolled_tasks=32):
    n = proj.shape[0]
    tk = vt.shape[-1]
    n_chunks = seq_len // tk
    q_tiles = seq_len // tq
    assert n_chunks % 2 == 0
    heads = DIFF_HEADS if DIFF_HEADS * n_chunks <= max_unrolled_tasks else 1
    width = heads * DIFF_V_DIM
    return pl.pallas_call(
        functools.partial(_diff_kernel, tk=tk, heads=heads, stabilize=stabilize),
        grid=(batch, DIFF_HEADS // heads, q_tiles),
        in_specs=[
            pl.BlockSpec(memory_space=pltpu.SMEM),
            pl.BlockSpec((tq, width), lambda b, h, i: (b * q_tiles + i, QB_OFF // width + h)),
            pl.BlockSpec((seq_len, width), lambda b, h, i: (b, KB_OFF // width + h)),
            pl.BlockSpec((None, heads, n_chunks, DIFF_VT_ROWS, tk), lambda b, h, i: (b, h, 0, 0, 0)),
            pl.BlockSpec((DIFF_V_DIM, 1), lambda b, h, i: (0, 0)),
        ],
        out_specs=pl.BlockSpec((tq, width), lambda b, h, i: (b * q_tiles + i, h)),
        out_shape=jax.ShapeDtypeStruct((n, DIFF_HEADS * DIFF_V_DIM), BF16),
        scratch_shapes=[
            pltpu.VMEM((heads, DIFF_V_DIM, 2 * tq), BF16),
            pltpu.VMEM((tk, 2 * tq), F32),
            pltpu.VMEM((tk, 2 * tq), F32),
            pltpu.VMEM((1, 2 * tq), F32),
            pltpu.VMEM((1, 2 * tq), F32),
            pltpu.VMEM((tk, 2 * tq), BF16),
            pltpu.VMEM((tk, 2 * tq), BF16),
            pltpu.VMEM((1, 2 * tq), F32),
            pltpu.VMEM((1, 2 * tq), F32),
            pltpu.VMEM((heads, 1, 2 * tq), F32),
            pltpu.VMEM((heads, DIFF_VT_ROWS, 2 * tq), F32),
        ],
        compiler_params=_params("parallel", "parallel", "arbitrary"),
        name="diffattn",
    )(scalars, proj, proj, vt, subln_col)


def _merge_kernel(x_ref, ya_ref, yb_ref, ga_ref, gb_ref, wa_ref, wb_ref, wo_ref, o_ref):
    a = jnp.dot(ya_ref[...], wa_ref[...], preferred_element_type=F32)
    b = jnp.dot(yb_ref[...], wb_ref[...], preferred_element_type=F32)
    merged = ga_ref[...].astype(F32) * a + gb_ref[...].astype(F32) * b
    o_ref[...] = x_ref[...] + jnp.dot(merged.astype(BF16), wo_ref[...], preferred_element_type=F32)


def _merge(x, ya, yb, proj, wa, wb, wo, *, tm=512):
    n = x.shape[0]
    return pl.pallas_call(
        _merge_kernel,
        grid=(n // tm,),
        in_specs=[
            pl.BlockSpec((tm, D_MODEL), lambda i: (i, 0)),
            pl.BlockSpec((tm, NA_WIDTH), lambda i: (i, 0)),
            pl.BlockSpec((tm, NA_WIDTH), lambda i: (i, 0)),
            pl.BlockSpec((tm, D_MODEL), lambda i: (i, GA_OFF // D_MODEL)),
            pl.BlockSpec((tm, D_MODEL), lambda i: (i, GB_OFF // D_MODEL)),
            pl.BlockSpec((NA_WIDTH, D_MODEL), lambda i: (0, 0)),
            pl.BlockSpec((NA_WIDTH, D_MODEL), lambda i: (0, 0)),
            pl.BlockSpec((D_MODEL, D_MODEL), lambda i: (0, 0)),
        ],
        out_specs=pl.BlockSpec((tm, D_MODEL), lambda i: (i, 0)),
        out_shape=jax.ShapeDtypeStruct((n, D_MODEL), F32),
        compiler_params=_params("parallel"),
        name="merge",
    )(x, ya, yb, proj, proj, wa, wb, wo)


def _rope_tables(seq_len):
    half = ROPE_DIM // 2
    inv_freq = jnp.power(ROPE_THETA, -jnp.arange(half, dtype=F32) * 2.0 / ROPE_DIM)
    ang = jnp.arange(seq_len, dtype=F32)[:, None] * inv_freq[None, :]
    cos, sin = jnp.cos(ang), jnp.sin(ang)
    ones = jnp.ones((seq_len, HEAD_DIM - ROPE_DIM), F32)
    zeros = jnp.zeros((seq_len, HEAD_DIM - ROPE_DIM), F32)
    zh = jnp.zeros_like(sin)
    c = jnp.concatenate([cos, cos, ones], axis=1)
    sa = jnp.concatenate([-sin, zh, zeros], axis=1)
    sb = jnp.concatenate([zh, sin, zeros], axis=1)
    reps = V7X_LANES // HEAD_DIM
    return tuple(jnp.tile(t, (1, reps)) for t in (c, sa, sb))


def _layer_consts(l, w):
    scale = HEAD_DIM ** -0.5
    heads = PROJ_BLOCK // HEAD_DIM
    head_gains = jnp.stack([
        jnp.tile(w["qa_norm"][l] * scale, heads), jnp.tile(w["ka_norm"][l], heads),
        jnp.tile(w["qb_norm"][l] * (scale * LOG2_E), heads), jnp.tile(w["kb_norm"][l], heads)])
    blk = np.arange(PROJ_BLOCK) // HEAD_DIM
    blockdiag = jnp.asarray((blk[:, None] == blk[None, :]).astype(np.float32) / HEAD_DIM, BF16)
    lam_init = 0.8 - 0.6 * math.exp(-0.3 * l)
    lam = (jnp.exp(jnp.sum(w["lam_q1"][l] * w["lam_k1"][l]))
           - jnp.exp(jnp.sum(w["lam_q2"][l] * w["lam_k2"][l])) + lam_init)
    score_bound = (HEAD_DIM * NORM_SLACK * jnp.max(jnp.abs(w["qb_norm"][l])) * (scale * LOG2_E)
                   * jnp.max(jnp.abs(w["kb_norm"][l])))
    return dict(
        score_bound=score_bound,
        ffn1_g=w["ffn1_norm"][l][None], ffn1_wi=w["ffn1_wi"][l].astype(BF16), ffn1_wo=w["ffn1_wo"][l].astype(BF16),
        mix_g=w["mix_norm"][l][None], w_in=w["w_in"][l].astype(BF16),
        head_gains=head_gains, blockdiag=blockdiag, bias=_natten_bias(w["rpb"][l]),
        scalars=jnp.stack([lam, jnp.asarray(1.0 - lam_init, F32)]).astype(F32),
        subln=w["subln"][l][:, None],
        w_a=w["w_a_out"][l].astype(BF16), w_b=w["w_b_out"][l].astype(BF16), w_o=w["w_o"][l].astype(BF16),
        ffn2_g=w["ffn2_norm"][l][None], ffn2_wi=w["ffn2_wi"][l].astype(BF16), ffn2_wo=w["ffn2_wo"][l].astype(BF16),
    )


def _encoder_layer(x, c, rope, batch, seq_len):
    x = _ffn(x, c["ffn1_g"], c["ffn1_wi"], c["ffn1_wo"])
    tk = _diff_key_chunk(seq_len)
    proj, vt = _in_proj(x, c["mix_g"], c["w_in"], c["head_gains"], c["blockdiag"], *rope, batch, seq_len,
                        tm=max(tk, PROJ_BLOCK), tk=tk)
    ya = _natten(proj, c["bias"], batch, seq_len)
    yb = lax.cond(
        c["score_bound"] <= DIFF_MAX_UNSTABILIZED_SCORE,
        lambda: _diffattn(proj, vt, c["scalars"], c["subln"], batch, seq_len, stabilize=False),
        lambda: _diffattn(proj, vt, c["scalars"], c["subln"], batch, seq_len, stabilize=True))
    x = _merge(x, ya, yb, proj, c["w_a"], c["w_b"], c["w_o"])
    return _ffn(x, c["ffn2_g"], c["ffn2_wi"], c["ffn2_wo"])


def _encoder(groups, weights, depth):
    consts = [_layer_consts(l, weights) for l in range(depth)]
    outs = []
    for x in groups:
        batch, seq_len, _ = x.shape
        rope = _rope_tables(seq_len)
        y = x.reshape(batch * seq_len, D_MODEL)
        for c in consts:
            y = _encoder_layer(y, c, rope, batch, seq_len)
        outs.append(y.reshape(x.shape))
    return tuple(outs)


def kernel(x_prompt, x_sample, ffn1_norm, ffn1_wi, ffn1_wo, mix_norm, w_in, qa_norm, ka_norm, rpb, qb_norm, kb_norm, lam_q1, lam_k1, lam_q2, lam_k2, subln, w_a_out, w_b_out, w_o, ffn2_norm, ffn2_wi, ffn2_wo):
    weights = dict(ffn1_norm=ffn1_norm, ffn1_wi=ffn1_wi, ffn1_wo=ffn1_wo, mix_norm=mix_norm, w_in=w_in,
                   qa_norm=qa_norm, ka_norm=ka_norm, rpb=rpb, qb_norm=qb_norm, kb_norm=kb_norm,
                   lam_q1=lam_q1, lam_k1=lam_k1, lam_q2=lam_q2, lam_k2=lam_k2, subln=subln,
                   w_a_out=w_a_out, w_b_out=w_b_out, w_o=w_o,
                   ffn2_norm=ffn2_norm, ffn2_wi=ffn2_wi, ffn2_wo=ffn2_wo)
    return _encoder((x_prompt, x_sample), weights, ffn1_norm.shape[0])
```

```python
import functools
import math

import jax
import jax.numpy as jnp
import numpy as np
from jax import lax
from jax.experimental import pallas as pl
from jax.experimental.pallas import tpu as pltpu

F32 = jnp.float32
BF16 = jnp.bfloat16

D_MODEL = 1024
GRID_W = 64
HEAD_DIM = 64
NA_HEADS = 8
DIFF_HEADS = 4
NA_WIDTH = NA_HEADS * HEAD_DIM
DIFF_V_DIM = 2 * HEAD_DIM
IN_WIDTH = 5120
D_FF = 2816
WIN_H = 8
WIN_W = 16
ROPE_THETA = 500000.0
ROPE_DIM = HEAD_DIM // 4
NORM_EPS = 1e-6
SUBLN_EPS = 1e-5
NEG_INF = -1e30
FFN_RES = 0.5

V7X_VMEM_BYTES = 64 * 1024 * 1024
V7X_LANES = 128
VMEM_LIMIT = V7X_VMEM_BYTES * 7 // 8

PROJ_BLOCK = 512
W_QA, W_KA, W_VA, W_QB, W_KB, W_VB, W_GA, W_GB = 0, 512, 1024, 1536, 2048, 2560, 3072, 4096
GA_OFF, GB_OFF, QA_OFF, KA_OFF, VA_OFF, QB_OFF, KB_OFF = 0, 1024, 2048, 2560, 3072, 3584, 4096
PROJ_OUT = 4608
DIFF_TK_CHOICES = (1024, 512, 256)
DIFF_MIN_CHUNKS = 8
BF16_SUBLANES = 16
DIFF_VT_ROWS = DIFF_V_DIM + BF16_SUBLANES
LOG2_E = math.log2(math.e)
NORM_SLACK = 1.05
DIFF_MAX_UNSTABILIZED_SCORE = 32.0


def _params(*semantics):
    return pltpu.CompilerParams(dimension_semantics=semantics, vmem_limit_bytes=VMEM_LIMIT)


def _rms(x, g, eps):
    return x * lax.rsqrt(jnp.mean(x * x, axis=-1, keepdims=True) + eps) * g


def _ffn_step(x, g_ref, wi_ref, wo_ref, chunk):
    h = _rms(x, g_ref[...], NORM_EPS).astype(BF16)
    n_chunks = D_FF // chunk

    def up(c):
        a = jnp.dot(h, wi_ref[:, c * chunk:(c + 1) * chunk], preferred_element_type=F32)
        b = jnp.dot(h, wi_ref[:, D_FF + c * chunk:D_FF + (c + 1) * chunk], preferred_element_type=F32)
        return a, b

    acc = jnp.zeros(x.shape, F32)
    nxt = up(0)
    for c in range(n_chunks):
        a, b = nxt
        if c + 1 < n_chunks:
            nxt = up(c + 1)
        gate = (a * jax.nn.sigmoid(a) * b).astype(BF16)
        acc = acc + jnp.dot(gate, wo_ref[c * chunk:(c + 1) * chunk, :], preferred_element_type=F32)
    return x + FFN_RES * acc


def _ffn_kernel(x_ref, g_ref, wi_ref, wo_ref, o_ref, *, chunk):
    o_ref[...] = _ffn_step(x_ref[...], g_ref, wi_ref, wo_ref, chunk)


def _ffn(x, g, wi, wo, *, tm=512, chunk=256):
    n = x.shape[0]
    return pl.pallas_call(
        functools.partial(_ffn_kernel, chunk=chunk),
        grid=(n // tm,),
        in_specs=[
            pl.BlockSpec((tm, D_MODEL), lambda i: (i, 0)),
            pl.BlockSpec((1, D_MODEL), lambda i: (0, 0)),
            pl.BlockSpec((D_MODEL, 2 * D_FF), lambda i: (0, 0)),
            pl.BlockSpec((D_FF, D_MODEL), lambda i: (0, 0)),
        ],
        out_specs=pl.BlockSpec((tm, D_MODEL), lambda i: (i, 0)),
        out_shape=jax.ShapeDtypeStruct((n, D_MODEL), F32),
        compiler_params=_params("parallel"),
        name="ffn",
    )(x, g, wi, wo)


def _proj_kernel(x_ref, g_ref, w_ref, hg_ref, bd_ref, cos_ref, sa_ref, sb_ref, o_ref, vt_ref):
    h = _rms(x_ref[...], g_ref[...], NORM_EPS).astype(BF16)
    tk = vt_ref.shape[-1]

    def proj(w_off):
        return jnp.dot(h, w_ref[:, w_off:w_off + PROJ_BLOCK], preferred_element_type=F32)

    def head_norm(p, gain_row):
        ms = jnp.dot((p * p).astype(BF16), bd_ref[...], preferred_element_type=F32)
        return p * lax.rsqrt(ms + NORM_EPS) * hg_ref[gain_row:gain_row + 1, :]

    reps = PROJ_BLOCK // V7X_LANES
    cos = jnp.concatenate([cos_ref[...]] * reps, axis=1)
    sa = jnp.concatenate([sa_ref[...]] * reps, axis=1)
    sb = jnp.concatenate([sb_ref[...]] * reps, axis=1)

    def rope(y):
        half = ROPE_DIM // 2
        up = pltpu.roll(y, PROJ_BLOCK - half, 1)
        dn = pltpu.roll(y, half, 1)
        return y * cos + up * sa + dn * sb

    def put(off, val):
        o_ref[:, off:off + PROJ_BLOCK] = val.astype(o_ref.dtype)

    def put_diff_values(vb):
        pad_row = lax.broadcasted_iota(jnp.int32, (BF16_SUBLANES, tk), 0)
        ones_rows = jnp.where(pad_row == 0, 1.0, 0.0).astype(vt_ref.dtype)
        for head in range(DIFF_HEADS):
            for c in range(vt_ref.shape[1]):
                blk = vb[c * tk:(c + 1) * tk, head * DIFF_V_DIM:(head + 1) * DIFF_V_DIM]
                vt_ref[head, c, :DIFF_V_DIM, :] = blk.T.astype(vt_ref.dtype)
                vt_ref[head, c, DIFF_V_DIM:, :] = ones_rows

    blocks = [
        (W_QA, lambda p: put(QA_OFF, head_norm(p, 0))),
        (W_GA, lambda p: put(GA_OFF, jax.nn.sigmoid(p))),
        (W_KA, lambda p: put(KA_OFF, head_norm(p, 1))),
        (W_GA + PROJ_BLOCK, lambda p: put(GA_OFF + PROJ_BLOCK, jax.nn.sigmoid(p))),
        (W_QB, lambda p: put(QB_OFF, rope(head_norm(p, 2)))),
        (W_GB, lambda p: put(GB_OFF, jax.nn.sigmoid(p))),
        (W_KB, lambda p: put(KB_OFF, rope(head_norm(p, 3)))),
        (W_GB + PROJ_BLOCK, lambda p: put(GB_OFF + PROJ_BLOCK, jax.nn.sigmoid(p))),
        (W_VB, put_diff_values),
        (W_VA, lambda p: put(VA_OFF, p)),
    ]
    nxt = proj(blocks[0][0])
    for n, (_, epilogue) in enumerate(blocks):
        p = nxt
        if n + 1 < len(blocks):
            nxt = proj(blocks[n + 1][0])
        epilogue(p)


def _diff_key_chunk(seq_len):
    fits = [t for t in DIFF_TK_CHOICES if seq_len % (2 * t) == 0]
    deep = [t for t in fits if seq_len // t >= DIFF_MIN_CHUNKS]
    return (deep or fits[-1:])[0]


def _in_proj(x, g, w, head_gains, blockdiag, cos, sa, sb, batch, seq_len, *, tm, tk):
    n = x.shape[0]
    tiles_per_seq = seq_len // tm
    return pl.pallas_call(
        _proj_kernel,
        grid=(n // tm,),
        in_specs=[
            pl.BlockSpec((tm, D_MODEL), lambda i: (i, 0)),
            pl.BlockSpec((1, D_MODEL), lambda i: (0, 0)),
            pl.BlockSpec((D_MODEL, IN_WIDTH), lambda i: (0, 0)),
            pl.BlockSpec((4, PROJ_BLOCK), lambda i: (0, 0)),
            pl.BlockSpec((PROJ_BLOCK, PROJ_BLOCK), lambda i: (0, 0)),
            pl.BlockSpec((tm, V7X_LANES), lambda i: (i % tiles_per_seq, 0)),
            pl.BlockSpec((tm, V7X_LANES), lambda i: (i % tiles_per_seq, 0)),
            pl.BlockSpec((tm, V7X_LANES), lambda i: (i % tiles_per_seq, 0)),
        ],
        out_specs=[
            pl.BlockSpec((tm, PROJ_OUT), lambda i: (i, 0)),
            pl.BlockSpec((None, DIFF_HEADS, tm // tk, DIFF_VT_ROWS, tk),
                         lambda i: (i // tiles_per_seq, 0, i % tiles_per_seq, 0, 0)),
        ],
        out_shape=[
            jax.ShapeDtypeStruct((n, PROJ_OUT), BF16),
            jax.ShapeDtypeStruct((batch, DIFF_HEADS, seq_len // tk, DIFF_VT_ROWS, tk), BF16),
        ],
        compiler_params=_params("parallel"),
        name="in_proj",
    )(x, g, w, head_gains, blockdiag, cos, sa, sb)


def _natten_kernel(q_ref, k_ref, v_ref, bias_ref, o_ref, *, rows, rows_per_step):
    n_keys = WIN_H * GRID_W
    r0 = pl.program_id(1) * rows_per_step
    block_start = jnp.clip(r0 - WIN_H // 2, 0, rows - (rows_per_step + WIN_H - 1))
    lane = lax.broadcasted_iota(jnp.int32, (GRID_W, V7X_LANES), 1)
    lo = lane < HEAD_DIM
    tasks = [(i, pair) for i in range(rows_per_step) for pair in range(NA_HEADS // 2)]

    def key_rows(i):
        start = jnp.clip(r0 + i - WIN_H // 2, 0, rows - WIN_H)
        variant = start - (r0 + i) + WIN_H - 1
        return pl.ds(pl.multiple_of((start - block_start) * GRID_W, GRID_W), n_keys), variant

    def scores(i, pair):
        keys, variant = key_rows(i)
        cols = slice(pair * V7X_LANES, (pair + 1) * V7X_LANES)
        qp = q_ref[i * GRID_W:(i + 1) * GRID_W, cols]
        zero = jnp.zeros_like(qp)
        lhs = jnp.concatenate([jnp.where(lo, qp, zero), jnp.where(lo, zero, qp)], axis=0)
        s = lax.dot_general(lhs, k_ref[keys, cols], (((1,), (1,)), ((), ())), preferred_element_type=F32)
        return s + bias_ref[variant, pair * 2 * GRID_W:(pair + 1) * 2 * GRID_W, :]

    s_next = scores(*tasks[0])
    for n, (i, pair) in enumerate(tasks):
        s = s_next
        if n + 1 < len(tasks):
            s_next = scores(*tasks[n + 1])
        keys, _ = key_rows(i)
        cols = slice(pair * V7X_LANES, (pair + 1) * V7X_LANES)
        m = jnp.max(s, axis=-1, keepdims=True)
        e = jnp.exp(s - m)
        l = jnp.sum(e, axis=-1, keepdims=True)
        o = jnp.dot(e.astype(BF16), v_ref[keys, cols], preferred_element_type=F32)
        o = o * (1.0 / l)
        o_ref[i * GRID_W:(i + 1) * GRID_W, cols] = jnp.where(lo, o[:GRID_W], o[GRID_W:]).astype(o_ref.dtype)


def _natten(proj, bias, batch, seq_len, *, rows_per_step=4):
    rows = seq_len // GRID_W
    n = proj.shape[0]
    block_rows = rows_per_step + WIN_H - 1
    assert rows % rows_per_step == 0 and rows >= block_rows
    steps = rows // rows_per_step

    def block_start(j):
        return jnp.clip(j * rows_per_step - WIN_H // 2, 0, rows - block_rows)

    return pl.pallas_call(
        functools.partial(_natten_kernel, rows=rows, rows_per_step=rows_per_step),
        grid=(batch, steps),
        in_specs=[
            pl.BlockSpec((rows_per_step * GRID_W, NA_WIDTH), lambda b, j: (b * steps + j, QA_OFF // NA_WIDTH)),
            pl.BlockSpec((pl.Element(block_rows * GRID_W), pl.Element(NA_WIDTH)),
                         lambda b, j: ((b * rows + block_start(j)) * GRID_W, KA_OFF)),
            pl.BlockSpec((pl.Element(block_rows * GRID_W), pl.Element(NA_WIDTH)),
                         lambda b, j: ((b * rows + block_start(j)) * GRID_W, VA_OFF)),
            pl.BlockSpec((WIN_H, NA_HEADS * GRID_W, WIN_H * GRID_W), lambda b, j: (0, 0, 0)),
        ],
        out_specs=pl.BlockSpec((rows_per_step * GRID_W, NA_WIDTH), lambda b, j: (b * steps + j, 0)),
        out_shape=jax.ShapeDtypeStruct((n, NA_WIDTH), BF16),
        compiler_params=_params("parallel", "arbitrary"),
        name="natten",
    )(proj, proj, proj, bias)


def _natten_bias(rpb):
    qc = np.arange(GRID_W)[:, None]
    kc = np.arange(GRID_W)[None, :]
    wstart = np.clip(qc - WIN_W // 2, 0, GRID_W - WIN_W)
    in_win = (kc >= wstart) & (kc < wstart + WIN_W)
    dc = np.clip(kc - qc + WIN_W - 1, 0, 2 * WIN_W - 2)
    dr = np.arange(WIN_H)[:, None] + np.arange(WIN_H)[None, :]
    pick_dc = (dc[..., None] == np.arange(2 * WIN_W - 1)).astype(np.float32)
    pick_dr = (dr[..., None] == np.arange(2 * WIN_H - 1)).astype(np.float32)
    b = jnp.einsum("tsr,hrd,qkd->thqsk", pick_dr, rpb.astype(F32), pick_dc,
                   precision=lax.Precision.HIGHEST)
    b = jnp.where(in_win[None, None, :, None, :], b, NEG_INF)
    return b.reshape(WIN_H, NA_HEADS * GRID_W, WIN_H * GRID_W)


def _diff_kernel(sc_ref, q_ref, k_ref, vt_ref, g_ref, o_ref,
                 qst_scr, s_a, s_b, cm_a, cm_b, p_a, p_b, al_a, al_b, m_scr, acc_scr, *, tk, heads, stabilize):
    tq = q_ref.shape[0]
    n_chunks = k_ref.shape[0] // tk
    bufs = ((s_a, cm_a, p_a, al_a), (s_b, cm_b, p_b, al_b))

    def head_cols(h):
        return slice(h * DIFF_V_DIM, (h + 1) * DIFF_V_DIM)

    def prepare(h):
        qt = q_ref[:, head_cols(h)].astype(F32).T
        row = lax.broadcasted_iota(jnp.int32, qt.shape, 0)
        zero = jnp.zeros_like(qt)
        qst_scr[h] = jnp.concatenate(
            [jnp.where(row < HEAD_DIM, qt, zero), jnp.where(row < HEAD_DIM, zero, qt)], axis=1).astype(BF16)
        m_scr[h] = jnp.full(m_scr.shape[1:], NEG_INF if stabilize else 0.0, F32)
        acc_scr[h] = jnp.zeros(acc_scr.shape[1:], F32)

    def scores(h, c, buf):
        s_ref, cm_ref, p_ref, _ = buf
        start = pl.multiple_of(c * tk, tk)
        s = jnp.dot(k_ref[pl.ds(start, tk), head_cols(h)], qst_scr[h], preferred_element_type=F32)
        if stabilize:
            s_ref[...] = s
            cm_ref[...] = jnp.max(s, axis=0, keepdims=True)
        else:
            p = jnp.exp2(s)
            p_ref[...] = p.astype(BF16)
            m_scr[h] = m_scr[h] + jnp.sum(p, axis=0, keepdims=True)

    def softmax(h, buf):
        s_ref, cm_ref, p_ref, al_ref = buf
        m_prev = m_scr[h]
        m_new = jnp.maximum(m_prev, cm_ref[...])
        al_ref[...] = jnp.exp2(m_prev - m_new)
        m_scr[h] = m_new
        p_ref[...] = jnp.exp2(s_ref[...] - m_new).astype(BF16)

    def accumulate(h, c, buf):
        _, _, p_ref, al_ref = buf
        if stabilize:
            pv = jnp.dot(vt_ref[h, c], p_ref[...], preferred_element_type=F32)
            acc_scr[h] = al_ref[...] * acc_scr[h] + pv
        else:
            pv = jnp.dot(vt_ref[h, c, :DIFF_V_DIM, :], p_ref[...], preferred_element_type=F32)
            acc_scr[h, :DIFF_V_DIM, :] = acc_scr[h, :DIFF_V_DIM, :] + pv

    def finalize(h):
        acc = acc_scr[h]
        total = acc[DIFF_V_DIM:DIFF_V_DIM + 1] if stabilize else m_scr[h]
        o = acc[:DIFF_V_DIM] * (1.0 / total)
        d = o[:, :tq] - sc_ref[0] * o[:, tq:]
        ms = jnp.mean(d * d, axis=0, keepdims=True)
        y = d * lax.rsqrt(ms + SUBLN_EPS) * (g_ref[...] * sc_ref[1])
        o_ref[:, head_cols(h)] = y.T.astype(o_ref.dtype)

    def turn(h, c, parity, nxt, prev):
        if nxt is not None:
            if nxt[0] != h:
                prepare(nxt[0])
            scores(nxt[0], nxt[1], bufs[1 - parity])
        if not stabilize:
            accumulate(h, c, bufs[parity])
            if nxt is None or nxt[0] != h:
                finalize(h)
            return
        softmax(h, bufs[parity])
        if prev is not None:
            accumulate(prev[0], prev[1], bufs[1 - parity])
            if prev[0] != h:
                finalize(prev[0])

    prepare(0)
    scores(0, 0, bufs[0])
    if heads == 1 and n_chunks > 4 and (n_chunks - 4) % 4 == 0:
        turn(0, 0, 0, (0, 1), None)
        turn(0, 1, 1, (0, 2), (0, 0))

        def body(i, carry):
            for j in range(4):
                c = 2 + 4 * i + j
                turn(0, c, j % 2, (0, c + 1), (0, c - 1))
            return carry
        lax.fori_loop(0, (n_chunks - 4) // 4, body, 0)
        turn(0, n_chunks - 2, 0, (0, n_chunks - 1), (0, n_chunks - 3))
        turn(0, n_chunks - 1, 1, None, (0, n_chunks - 2))
    else:
        tasks = [(h, c) for h in range(heads) for c in range(n_chunks)]
        for t, (h, c) in enumerate(tasks):
            turn(h, c, t % 2, tasks[t + 1] if t + 1 < len(tasks) else None, tasks[t - 1] if t else None)
    if stabilize:
        accumulate(heads - 1, n_chunks - 1, bufs[1])
        finalize(heads - 1)


def _diffattn(proj, vt, scalars, subln_col, batch, seq_len, *, stabilize, tq=512, max_unrolled_tasks=32):
    n = proj.shape[0]
    tk = vt.shape[-1]
    n_chunks = seq_len // tk
    q_tiles = seq_len // tq
    assert n_chunks % 2 == 0
    heads = DIFF_HEADS if DIFF_HEADS * n_chunks <= max_unrolled_tasks else 1
    width = heads * DIFF_V_DIM
    return pl.pallas_call(
        functools.partial(_diff_kernel, tk=tk, heads=heads, stabilize=stabilize),
        grid=(batch, DIFF_HEADS // heads, q_tiles),
        in_specs=[
            pl.BlockSpec(memory_space=pltpu.SMEM),
            pl.BlockSpec((tq, width), lambda b, h, i: (b * q_tiles + i, QB_OFF // width + h)),
            pl.BlockSpec((seq_len, width), lambda b, h, i: (b, KB_OFF // width + h)),
            pl.BlockSpec((None, heads, n_chunks, DIFF_VT_ROWS, tk), lambda b, h, i: (b, h, 0, 0, 0)),
            pl.BlockSpec((DIFF_V_DIM, 1), lambda b, h, i: (0, 0)),
        ],
        out_specs=pl.BlockSpec((tq, width), lambda b, h, i: (b * q_tiles + i, h)),
        out_shape=jax.ShapeDtypeStruct((n, DIFF_HEADS * DIFF_V_DIM), BF16),
        scratch_shapes=[
            pltpu.VMEM((heads, DIFF_V_DIM, 2 * tq), BF16),
            pltpu.VMEM((tk, 2 * tq), F32),
            pltpu.VMEM((tk, 2 * tq), F32),
            pltpu.VMEM((1, 2 * tq), F32),
            pltpu.VMEM((1, 2 * tq), F32),
            pltpu.VMEM((tk, 2 * tq), BF16),
            pltpu.VMEM((tk, 2 * tq), BF16),
            pltpu.VMEM((1, 2 * tq), F32),
            pltpu.VMEM((1, 2 * tq), F32),
            pltpu.VMEM((heads, 1, 2 * tq), F32),
            pltpu.VMEM((heads, DIFF_VT_ROWS, 2 * tq), F32),
        ],
        compiler_params=_params("parallel", "parallel", "arbitrary"),
        name="diffattn",
    )(scalars, proj, proj, vt, subln_col)


def _merge_ffn_kernel(x_ref, ya_ref, yb_ref, ga_ref, gb_ref, wa_ref, wb_ref, wo_ref,
                      g_ref, ffn_wi_ref, ffn_wo_ref, o_ref, *, chunk):
    a = jnp.dot(ya_ref[...], wa_ref[...], preferred_element_type=F32)
    b = jnp.dot(yb_ref[...], wb_ref[...], preferred_element_type=F32)
    merged = ga_ref[...].astype(F32) * a + gb_ref[...].astype(F32) * b
    x = x_ref[...] + jnp.dot(merged.astype(BF16), wo_ref[...], preferred_element_type=F32)
    o_ref[...] = _ffn_step(x, g_ref, ffn_wi_ref, ffn_wo_ref, chunk)


def _merge_ffn(x, ya, yb, proj, wa, wb, wo, g, ffn_wi, ffn_wo, *, tm=512, chunk=256):
    n = x.shape[0]
    return pl.pallas_call(
        functools.partial(_merge_ffn_kernel, chunk=chunk),
        grid=(n // tm,),
        in_specs=[
            pl.BlockSpec((tm, D_MODEL), lambda i: (i, 0)),
            pl.BlockSpec((tm, NA_WIDTH), lambda i: (i, 0)),
            pl.BlockSpec((tm, NA_WIDTH), lambda i: (i, 0)),
            pl.BlockSpec((tm, D_MODEL), lambda i: (i, GA_OFF // D_MODEL)),
            pl.BlockSpec((tm, D_MODEL), lambda i: (i, GB_OFF // D_MODEL)),
            pl.BlockSpec((NA_WIDTH, D_MODEL), lambda i: (0, 0)),
            pl.BlockSpec((NA_WIDTH, D_MODEL), lambda i: (0, 0)),
            pl.BlockSpec((D_MODEL, D_MODEL), lambda i: (0, 0)),
            pl.BlockSpec((1, D_MODEL), lambda i: (0, 0)),
            pl.BlockSpec((D_MODEL, 2 * D_FF), lambda i: (0, 0)),
            pl.BlockSpec((D_FF, D_MODEL), lambda i: (0, 0)),
        ],
        out_specs=pl.BlockSpec((tm, D_MODEL), lambda i: (i, 0)),
        out_shape=jax.ShapeDtypeStruct((n, D_MODEL), F32),
        compiler_params=_params("parallel"),
        name="merge_ffn",
    )(x, ya, yb, proj, proj, wa, wb, wo, g, ffn_wi, ffn_wo)


def _rope_tables(seq_len):
    half = ROPE_DIM // 2
    inv_freq = jnp.power(ROPE_THETA, -jnp.arange(half, dtype=F32) * 2.0 / ROPE_DIM)
    ang = jnp.arange(seq_len, dtype=F32)[:, None] * inv_freq[None, :]
    cos, sin = jnp.cos(ang), jnp.sin(ang)
    ones = jnp.ones((seq_len, HEAD_DIM - ROPE_DIM), F32)
    zeros = jnp.zeros((seq_len, HEAD_DIM - ROPE_DIM), F32)
    zh = jnp.zeros_like(sin)
    c = jnp.concatenate([cos, cos, ones], axis=1)
    sa = jnp.concatenate([-sin, zh, zeros], axis=1)
    sb = jnp.concatenate([zh, sin, zeros], axis=1)
    reps = V7X_LANES // HEAD_DIM
    return tuple(jnp.tile(t, (1, reps)) for t in (c, sa, sb))


def _layer_consts(l, w):
    scale = HEAD_DIM ** -0.5
    heads = PROJ_BLOCK // HEAD_DIM
    head_gains = jnp.stack([
        jnp.tile(w["qa_norm"][l] * scale, heads), jnp.tile(w["ka_norm"][l], heads),
        jnp.tile(w["qb_norm"][l] * (scale * LOG2_E), heads), jnp.tile(w["kb_norm"][l], heads)])
    blk = np.arange(PROJ_BLOCK) // HEAD_DIM
    blockdiag = jnp.asarray((blk[:, None] == blk[None, :]).astype(np.float32) / HEAD_DIM, BF16)
    lam_init = 0.8 - 0.6 * math.exp(-0.3 * l)
    lam = (jnp.exp(jnp.sum(w["lam_q1"][l] * w["lam_k1"][l]))
           - jnp.exp(jnp.sum(w["lam_q2"][l] * w["lam_k2"][l])) + lam_init)
    score_bound = (HEAD_DIM * NORM_SLACK * jnp.max(jnp.abs(w["qb_norm"][l])) * (scale * LOG2_E)
                   * jnp.max(jnp.abs(w["kb_norm"][l])))
    return dict(
        score_bound=score_bound,
        ffn1_g=w["ffn1_norm"][l][None], ffn1_wi=w["ffn1_wi"][l].astype(BF16), ffn1_wo=w["ffn1_wo"][l].astype(BF16),
        mix_g=w["mix_norm"][l][None], w_in=w["w_in"][l].astype(BF16),
        head_gains=head_gains, blockdiag=blockdiag, bias=_natten_bias(w["rpb"][l]),
        scalars=jnp.stack([lam, jnp.asarray(1.0 - lam_init, F32)]).astype(F32),
        subln=w["subln"][l][:, None],
        w_a=w["w_a_out"][l].astype(BF16), w_b=w["w_b_out"][l].astype(BF16), w_o=w["w_o"][l].astype(BF16),
        ffn2_g=w["ffn2_norm"][l][None], ffn2_wi=w["ffn2_wi"][l].astype(BF16), ffn2_wo=w["ffn2_wo"][l].astype(BF16),
    )


def _encoder_layer(x, c, rope, batch, seq_len):
    x = _ffn(x, c["ffn1_g"], c["ffn1_wi"], c["ffn1_wo"])
    tk = _diff_key_chunk(seq_len)
    proj, vt = _in_proj(x, c["mix_g"], c["w_in"], c["head_gains"], c["blockdiag"], *rope, batch, seq_len,
                        tm=max(tk, PROJ_BLOCK), tk=tk)
    ya = _natten(proj, c["bias"], batch, seq_len)
    yb = lax.cond(
        c["score_bound"] <= DIFF_MAX_UNSTABILIZED_SCORE,
        lambda: _diffattn(proj, vt, c["scalars"], c["subln"], batch, seq_len, stabilize=False),
        lambda: _diffattn(proj, vt, c["scalars"], c["subln"], batch, seq_len, stabilize=True))
    return _merge_ffn(x, ya, yb, proj, c["w_a"], c["w_b"], c["w_o"], c["ffn2_g"], c["ffn2_wi"], c["ffn2_wo"])


def _encoder(groups, weights, depth):
    consts = [_layer_consts(l, weights) for l in range(depth)]
    outs = []
    for x in groups:
        batch, seq_len, _ = x.shape
        rope = _rope_tables(seq_len)
        y = x.reshape(batch * seq_len, D_MODEL)
        for c in consts:
            y = _encoder_layer(y, c, rope, batch, seq_len)
        outs.append(y.reshape(x.shape))
    return tuple(outs)


def kernel(x_prompt, x_sample, ffn1_norm, ffn1_wi, ffn1_wo, mix_norm, w_in, qa_norm, ka_norm, rpb, qb_norm, kb_norm, lam_q1, lam_k1, lam_q2, lam_k2, subln, w_a_out, w_b_out, w_o, ffn2_norm, ffn2_wi, ffn2_wo):
    weights = dict(ffn1_norm=ffn1_norm, ffn1_wi=ffn1_wi, ffn1_wo=ffn1_wo, mix_norm=mix_norm, w_in=w_in,
                   qa_norm=qa_norm, ka_norm=ka_norm, rpb=rpb, qb_norm=qb_norm, kb_norm=kb_norm,
                   lam_q1=lam_q1, lam_k1=lam_k1, lam_q2=lam_q2, lam_k2=lam_k2, subln=subln,
                   w_a_out=w_a_out, w_b_out=w_b_out, w_o=w_o,
                   ffn2_norm=ffn2_norm, ffn2_wi=ffn2_wi, ffn2_wo=ffn2_wo)
    return _encoder((x_prompt, x_sample), weights, ffn1_norm.shape[0])
```

```python
import functools
import math

import jax
import jax.numpy as jnp
import numpy as np
from jax import lax
from jax.experimental import pallas as pl
from jax.experimental.pallas import tpu as pltpu

F32 = jnp.float32
BF16 = jnp.bfloat16

D_MODEL = 1024
GRID_W = 64
HEAD_DIM = 64
NA_HEADS = 8
DIFF_HEADS = 4
NA_WIDTH = NA_HEADS * HEAD_DIM
DIFF_V_DIM = 2 * HEAD_DIM
IN_WIDTH = 5120
D_FF = 2816
WIN_H = 8
WIN_W = 16
ROPE_THETA = 500000.0
ROPE_DIM = HEAD_DIM // 4
NORM_EPS = 1e-6
SUBLN_EPS = 1e-5
NEG_INF = -1e30
FFN_RES = 0.5

V7X_VMEM_BYTES = 64 * 1024 * 1024
V7X_LANES = 128
VMEM_LIMIT = V7X_VMEM_BYTES * 7 // 8

PROJ_BLOCK = 512
W_QA, W_KA, W_VA, W_QB, W_KB, W_VB, W_GA, W_GB = 0, 512, 1024, 1536, 2048, 2560, 3072, 4096
GA_OFF, GB_OFF, QA_OFF, KA_OFF, VA_OFF, QB_OFF, KB_OFF = 0, 1024, 2048, 2560, 3072, 3584, 4096
PROJ_OUT = 4608
DIFF_TK_CHOICES = (1024, 512, 256)
DIFF_MIN_CHUNKS = 8
BF16_SUBLANES = 16
DIFF_VT_ROWS = DIFF_V_DIM + BF16_SUBLANES
LOG2_E = math.log2(math.e)
NORM_SLACK = 1.05
DIFF_MAX_UNSTABILIZED_SCORE = 32.0


def _params(*semantics):
    return pltpu.CompilerParams(dimension_semantics=semantics, vmem_limit_bytes=VMEM_LIMIT)


def _rms(x, g, eps):
    return x * lax.rsqrt(jnp.mean(x * x, axis=-1, keepdims=True) + eps) * g


def _ffn_step(x, g_ref, wi_ref, wo_ref, chunk):
    h = _rms(x, g_ref[...], NORM_EPS).astype(BF16)
    n_chunks = D_FF // chunk

    def up(c):
        a = jnp.dot(h, wi_ref[:, c * chunk:(c + 1) * chunk], preferred_element_type=F32)
        b = jnp.dot(h, wi_ref[:, D_FF + c * chunk:D_FF + (c + 1) * chunk], preferred_element_type=F32)
        return a, b

    acc = jnp.zeros(x.shape, F32)
    nxt = up(0)
    for c in range(n_chunks):
        a, b = nxt
        if c + 1 < n_chunks:
            nxt = up(c + 1)
        gate = (a * jax.nn.sigmoid(a) * b).astype(BF16)
        acc = acc + jnp.dot(gate, wo_ref[c * chunk:(c + 1) * chunk, :], preferred_element_type=F32)
    return x + FFN_RES * acc


def _ffn_proj_kernel(x_ref, ffn_g_ref, ffn_wi_ref, ffn_wo_ref, g_ref, w_ref, hg_ref, bd_ref,
                     cos_ref, sa_ref, sb_ref, x_out_ref, o_ref, vt_ref, *, chunk):
    x = _ffn_step(x_ref[...], ffn_g_ref, ffn_wi_ref, ffn_wo_ref, chunk)
    x_out_ref[...] = x
    h = _rms(x, g_ref[...], NORM_EPS).astype(BF16)
    tk = vt_ref.shape[-1]

    def proj(w_off):
        return jnp.dot(h, w_ref[:, w_off:w_off + PROJ_BLOCK], preferred_element_type=F32)

    def head_norm(p, gain_row):
        ms = jnp.dot((p * p).astype(BF16), bd_ref[...], preferred_element_type=F32)
        return p * lax.rsqrt(ms + NORM_EPS) * hg_ref[gain_row:gain_row + 1, :]

    reps = PROJ_BLOCK // V7X_LANES
    cos = jnp.concatenate([cos_ref[...]] * reps, axis=1)
    sa = jnp.concatenate([sa_ref[...]] * reps, axis=1)
    sb = jnp.concatenate([sb_ref[...]] * reps, axis=1)

    def rope(y):
        half = ROPE_DIM // 2
        up = pltpu.roll(y, PROJ_BLOCK - half, 1)
        dn = pltpu.roll(y, half, 1)
        return y * cos + up * sa + dn * sb

    def put(off, val):
        o_ref[:, off:off + PROJ_BLOCK] = val.astype(o_ref.dtype)

    def put_diff_values(vb):
        pad_row = lax.broadcasted_iota(jnp.int32, (BF16_SUBLANES, tk), 0)
        ones_rows = jnp.where(pad_row == 0, 1.0, 0.0).astype(vt_ref.dtype)
        for head in range(DIFF_HEADS):
            for c in range(vt_ref.shape[1]):
                blk = vb[c * tk:(c + 1) * tk, head * DIFF_V_DIM:(head + 1) * DIFF_V_DIM]
                vt_ref[head, c, :DIFF_V_DIM, :] = blk.T.astype(vt_ref.dtype)
                vt_ref[head, c, DIFF_V_DIM:, :] = ones_rows

    blocks = [
        (W_QA, lambda p: put(QA_OFF, head_norm(p, 0))),
        (W_GA, lambda p: put(GA_OFF, jax.nn.sigmoid(p))),
        (W_KA, lambda p: put(KA_OFF, head_norm(p, 1))),
        (W_GA + PROJ_BLOCK, lambda p: put(GA_OFF + PROJ_BLOCK, jax.nn.sigmoid(p))),
        (W_QB, lambda p: put(QB_OFF, rope(head_norm(p, 2)))),
        (W_GB, lambda p: put(GB_OFF, jax.nn.sigmoid(p))),
        (W_KB, lambda p: put(KB_OFF, rope(head_norm(p, 3)))),
        (W_GB + PROJ_BLOCK, lambda p: put(GB_OFF + PROJ_BLOCK, jax.nn.sigmoid(p))),
        (W_VB, put_diff_values),
        (W_VA, lambda p: put(VA_OFF, p)),
    ]
    nxt = proj(blocks[0][0])
    for n, (_, epilogue) in enumerate(blocks):
        p = nxt
        if n + 1 < len(blocks):
            nxt = proj(blocks[n + 1][0])
        epilogue(p)


def _diff_key_chunk(seq_len):
    fits = [t for t in DIFF_TK_CHOICES if seq_len % (2 * t) == 0]
    deep = [t for t in fits if seq_len // t >= DIFF_MIN_CHUNKS]
    return (deep or fits[-1:])[0]


def _ffn_proj(x, ffn_g, ffn_wi, ffn_wo, g, w, head_gains, blockdiag, cos, sa, sb, batch, seq_len, *,
              tk, tm=512, chunk=256):
    n = x.shape[0]
    tiles_per_seq = seq_len // tm
    if tm >= tk:
        vt_spec = pl.BlockSpec((None, DIFF_HEADS, tm // tk, DIFF_VT_ROWS, tk),
                               lambda i: (i // tiles_per_seq, 0, i % tiles_per_seq, 0, 0))
    else:
        sub = tk // tm
        vt_spec = pl.BlockSpec((None, DIFF_HEADS, 1, DIFF_VT_ROWS, tm),
                               lambda i: (i // tiles_per_seq, 0, (i % tiles_per_seq) // sub, 0,
                                          (i % tiles_per_seq) % sub))
    const = lambda i: (0, 0)
    return pl.pallas_call(
        functools.partial(_ffn_proj_kernel, chunk=chunk),
        grid=(n // tm,),
        in_specs=[
            pl.BlockSpec((tm, D_MODEL), lambda i: (i, 0)),
            pl.BlockSpec((1, D_MODEL), const),
            pl.BlockSpec((D_MODEL, 2 * D_FF), const),
            pl.BlockSpec((D_FF, D_MODEL), const),
            pl.BlockSpec((1, D_MODEL), const),
            pl.BlockSpec((D_MODEL, IN_WIDTH), const),
            pl.BlockSpec((4, PROJ_BLOCK), const),
            pl.BlockSpec((PROJ_BLOCK, PROJ_BLOCK), const),
            pl.BlockSpec((tm, V7X_LANES), lambda i: (i % tiles_per_seq, 0)),
            pl.BlockSpec((tm, V7X_LANES), lambda i: (i % tiles_per_seq, 0)),
            pl.BlockSpec((tm, V7X_LANES), lambda i: (i % tiles_per_seq, 0)),
        ],
        out_specs=[
            pl.BlockSpec((tm, D_MODEL), lambda i: (i, 0)),
            pl.BlockSpec((tm, PROJ_OUT), lambda i: (i, 0)),
            vt_spec,
        ],
        out_shape=[
            jax.ShapeDtypeStruct((n, D_MODEL), F32),
            jax.ShapeDtypeStruct((n, PROJ_OUT), BF16),
            jax.ShapeDtypeStruct((batch, DIFF_HEADS, seq_len // tk, DIFF_VT_ROWS, tk), BF16),
        ],
        compiler_params=_params("parallel"),
        name="ffn_proj",
    )(x, ffn_g, ffn_wi, ffn_wo, g, w, head_gains, blockdiag, cos, sa, sb)


def _natten_kernel(q_ref, k_ref, v_ref, bias_ref, o_ref, *, rows, rows_per_step):
    n_keys = WIN_H * GRID_W
    r0 = pl.program_id(1) * rows_per_step
    block_start = jnp.clip(r0 - WIN_H // 2, 0, rows - (rows_per_step + WIN_H - 1))
    lane = lax.broadcasted_iota(jnp.int32, (GRID_W, V7X_LANES), 1)
    lo = lane < HEAD_DIM
    tasks = [(i, pair) for i in range(rows_per_step) for pair in range(NA_HEADS // 2)]

    def key_rows(i):
        start = jnp.clip(r0 + i - WIN_H // 2, 0, rows - WIN_H)
        variant = start - (r0 + i) + WIN_H - 1
        return pl.ds(pl.multiple_of((start - block_start) * GRID_W, GRID_W), n_keys), variant

    def scores(i, pair):
        keys, variant = key_rows(i)
        cols = slice(pair * V7X_LANES, (pair + 1) * V7X_LANES)
        qp = q_ref[i * GRID_W:(i + 1) * GRID_W, cols]
        zero = jnp.zeros_like(qp)
        lhs = jnp.concatenate([jnp.where(lo, qp, zero), jnp.where(lo, zero, qp)], axis=0)
        s = lax.dot_general(lhs, k_ref[keys, cols], (((1,), (1,)), ((), ())), preferred_element_type=F32)
        return s + bias_ref[variant, pair * 2 * GRID_W:(pair + 1) * 2 * GRID_W, :]

    s_next = scores(*tasks[0])
    for n, (i, pair) in enumerate(tasks):
        s = s_next
        if n + 1 < len(tasks):
            s_next = scores(*tasks[n + 1])
        keys, _ = key_rows(i)
        cols = slice(pair * V7X_LANES, (pair + 1) * V7X_LANES)
        m = jnp.max(s, axis=-1, keepdims=True)
        e = jnp.exp(s - m)
        l = jnp.sum(e, axis=-1, keepdims=True)
        o = jnp.dot(e.astype(BF16), v_ref[keys, cols], preferred_element_type=F32)
        o = o * (1.0 / l)
        o_ref[i * GRID_W:(i + 1) * GRID_W, cols] = jnp.where(lo, o[:GRID_W], o[GRID_W:]).astype(o_ref.dtype)


def _natten(proj, bias, batch, seq_len, *, rows_per_step=8):
    rows = seq_len // GRID_W
    n = proj.shape[0]
    block_rows = rows_per_step + WIN_H - 1
    assert rows % rows_per_step == 0 and rows >= block_rows
    steps = rows // rows_per_step

    def block_start(j):
        return jnp.clip(j * rows_per_step - WIN_H // 2, 0, rows - block_rows)

    return pl.pallas_call(
        functools.partial(_natten_kernel, rows=rows, rows_per_step=rows_per_step),
        grid=(batch, steps),
        in_specs=[
            pl.BlockSpec((rows_per_step * GRID_W, NA_WIDTH), lambda b, j: (b * steps + j, QA_OFF // NA_WIDTH)),
            pl.BlockSpec((pl.Element(block_rows * GRID_W), pl.Element(NA_WIDTH)),
                         lambda b, j: ((b * rows + block_start(j)) * GRID_W, KA_OFF)),
            pl.BlockSpec((pl.Element(block_rows * GRID_W), pl.Element(NA_WIDTH)),
                         lambda b, j: ((b * rows + block_start(j)) * GRID_W, VA_OFF)),
            pl.BlockSpec((WIN_H, NA_HEADS * GRID_W, WIN_H * GRID_W), lambda b, j: (0, 0, 0)),
        ],
        out_specs=pl.BlockSpec((rows_per_step * GRID_W, NA_WIDTH), lambda b, j: (b * steps + j, 0)),
        out_shape=jax.ShapeDtypeStruct((n, NA_WIDTH), BF16),
        compiler_params=_params("parallel", "arbitrary"),
        name="natten",
    )(proj, proj, proj, bias)


def _natten_bias(rpb):
    qc = np.arange(GRID_W)[:, None]
    kc = np.arange(GRID_W)[None, :]
    wstart = np.clip(qc - WIN_W // 2, 0, GRID_W - WIN_W)
    in_win = (kc >= wstart) & (kc < wstart + WIN_W)
    dc = np.clip(kc - qc + WIN_W - 1, 0, 2 * WIN_W - 2)
    dr = np.arange(WIN_H)[:, None] + np.arange(WIN_H)[None, :]
    pick_dc = (dc[..., None] == np.arange(2 * WIN_W - 1)).astype(np.float32)
    pick_dr = (dr[..., None] == np.arange(2 * WIN_H - 1)).astype(np.float32)
    b = jnp.einsum("tsr,hrd,qkd->thqsk", pick_dr, rpb.astype(F32), pick_dc,
                   precision=lax.Precision.HIGHEST)
    b = jnp.where(in_win[None, None, :, None, :], b, NEG_INF)
    return b.reshape(WIN_H, NA_HEADS * GRID_W, WIN_H * GRID_W)


def _diff_kernel(sc_ref, q_ref, k_ref, vt_ref, g_ref, o_ref,
                 qst_scr, s_a, s_b, cm_a, cm_b, p_a, p_b, al_a, al_b, m_scr, acc_scr, *, tk, heads, stabilize):
    tq = q_ref.shape[0]
    n_chunks = k_ref.shape[0] // tk
    bufs = ((s_a, cm_a, p_a, al_a), (s_b, cm_b, p_b, al_b))

    def head_cols(h):
        return slice(h * DIFF_V_DIM, (h + 1) * DIFF_V_DIM)

    def prepare(h):
        qt = q_ref[:, head_cols(h)].astype(F32).T
        row = lax.broadcasted_iota(jnp.int32, qt.shape, 0)
        zero = jnp.zeros_like(qt)
        qst_scr[h] = jnp.concatenate(
            [jnp.where(row < HEAD_DIM, qt, zero), jnp.where(row < HEAD_DIM, zero, qt)], axis=1).astype(BF16)
        m_scr[h] = jnp.full(m_scr.shape[1:], NEG_INF if stabilize else 0.0, F32)
        acc_scr[h] = jnp.zeros(acc_scr.shape[1:], F32)

    def scores(h, c, buf):
        s_ref, cm_ref, p_ref, _ = buf
        start = pl.multiple_of(c * tk, tk)
        s = jnp.dot(k_ref[pl.ds(start, tk), head_cols(h)], qst_scr[h], preferred_element_type=F32)
        if stabilize:
            s_ref[...] = s
            cm_ref[...] = jnp.max(s, axis=0, keepdims=True)
        else:
            p = jnp.exp2(s)
            p_ref[...] = p.astype(BF16)
            m_scr[h] = m_scr[h] + jnp.sum(p, axis=0, keepdims=True)

    def softmax(h, buf):
        s_ref, cm_ref, p_ref, al_ref = buf
        m_prev = m_scr[h]
        m_new = jnp.maximum(m_prev, cm_ref[...])
        al_ref[...] = jnp.exp2(m_prev - m_new)
        m_scr[h] = m_new
        p_ref[...] = jnp.exp2(s_ref[...] - m_new).astype(BF16)

    def accumulate(h, c, buf):
        _, _, p_ref, al_ref = buf
        if stabilize:
            pv = jnp.dot(vt_ref[h, c], p_ref[...], preferred_element_type=F32)
            acc_scr[h] = al_ref[...] * acc_scr[h] + pv
        else:
            pv = jnp.dot(vt_ref[h, c, :DIFF_V_DIM, :], p_ref[...], preferred_element_type=F32)
            acc_scr[h, :DIFF_V_DIM, :] = acc_scr[h, :DIFF_V_DIM, :] + pv

    def finalize(h):
        acc = acc_scr[h]
        total = acc[DIFF_V_DIM:DIFF_V_DIM + 1] if stabilize else m_scr[h]
        o = acc[:DIFF_V_DIM] * (1.0 / total)
        d = o[:, :tq] - sc_ref[0] * o[:, tq:]
        ms = jnp.mean(d * d, axis=0, keepdims=True)
        y = d * lax.rsqrt(ms + SUBLN_EPS) * (g_ref[...] * sc_ref[1])
        o_ref[:, head_cols(h)] = y.T.astype(o_ref.dtype)

    def turn(h, c, parity, nxt, prev):
        if nxt is not None:
            if nxt[0] != h:
                prepare(nxt[0])
            scores(nxt[0], nxt[1], bufs[1 - parity])
        if not stabilize:
            accumulate(h, c, bufs[parity])
            if nxt is None or nxt[0] != h:
                finalize(h)
            return
        softmax(h, bufs[parity])
        if prev is not None:
            accumulate(prev[0], prev[1], bufs[1 - parity])
            if prev[0] != h:
                finalize(prev[0])

    prepare(0)
    scores(0, 0, bufs[0])
    if heads == 1 and n_chunks > 4 and (n_chunks - 4) % 4 == 0:
        turn(0, 0, 0, (0, 1), None)
        turn(0, 1, 1, (0, 2), (0, 0))

        def body(i, carry):
            for j in range(4):
                c = 2 + 4 * i + j
                turn(0, c, j % 2, (0, c + 1), (0, c - 1))
            return carry
        lax.fori_loop(0, (n_chunks - 4) // 4, body, 0)
        turn(0, n_chunks - 2, 0, (0, n_chunks - 1), (0, n_chunks - 3))
        turn(0, n_chunks - 1, 1, None, (0, n_chunks - 2))
    else:
        tasks = [(h, c) for h in range(heads) for c in range(n_chunks)]
        for t, (h, c) in enumerate(tasks):
            turn(h, c, t % 2, tasks[t + 1] if t + 1 < len(tasks) else None, tasks[t - 1] if t else None)
    if stabilize:
        accumulate(heads - 1, n_chunks - 1, bufs[1])
        finalize(heads - 1)


def _diffattn(proj, vt, scalars, subln_col, batch, seq_len, *, stabilize, tq=512, max_unrolled_tasks=32):
    n = proj.shape[0]
    tk = vt.shape[-1]
    n_chunks = seq_len // tk
    q_tiles = seq_len // tq
    assert n_chunks % 2 == 0
    heads = DIFF_HEADS if DIFF_HEADS * n_chunks <= max_unrolled_tasks else 1
    width = heads * DIFF_V_DIM
    return pl.pallas_call(
        functools.partial(_diff_kernel, tk=tk, heads=heads, stabilize=stabilize),
        grid=(batch, DIFF_HEADS // heads, q_tiles),
        in_specs=[
            pl.BlockSpec(memory_space=pltpu.SMEM),
            pl.BlockSpec((tq, width), lambda b, h, i: (b * q_tiles + i, QB_OFF // width + h)),
            pl.BlockSpec((seq_len, width), lambda b, h, i: (b, KB_OFF // width + h)),
            pl.BlockSpec((None, heads, n_chunks, DIFF_VT_ROWS, tk), lambda b, h, i: (b, h, 0, 0, 0)),
            pl.BlockSpec((DIFF_V_DIM, 1), lambda b, h, i: (0, 0)),
        ],
        out_specs=pl.BlockSpec((tq, width), lambda b, h, i: (b * q_tiles + i, h)),
        out_shape=jax.ShapeDtypeStruct((n, DIFF_HEADS * DIFF_V_DIM), BF16),
        scratch_shapes=[
            pltpu.VMEM((heads, DIFF_V_DIM, 2 * tq), BF16),
            pltpu.VMEM((tk, 2 * tq), F32),
            pltpu.VMEM((tk, 2 * tq), F32),
            pltpu.VMEM((1, 2 * tq), F32),
            pltpu.VMEM((1, 2 * tq), F32),
            pltpu.VMEM((tk, 2 * tq), BF16),
            pltpu.VMEM((tk, 2 * tq), BF16),
            pltpu.VMEM((1, 2 * tq), F32),
            pltpu.VMEM((1, 2 * tq), F32),
            pltpu.VMEM((heads, 1, 2 * tq), F32),
            pltpu.VMEM((heads, DIFF_VT_ROWS, 2 * tq), F32),
        ],
        compiler_params=_params("parallel", "parallel", "arbitrary"),
        name="diffattn",
    )(scalars, proj, proj, vt, subln_col)


def _merge_ffn_kernel(x_ref, ya_ref, yb_ref, ga_ref, gb_ref, wa_ref, wb_ref, wo_ref,
                      g_ref, ffn_wi_ref, ffn_wo_ref, o_ref, *, chunk):
    a = jnp.dot(ya_ref[...], wa_ref[...], preferred_element_type=F32)
    b = jnp.dot(yb_ref[...], wb_ref[...], preferred_element_type=F32)
    merged = ga_ref[...].astype(F32) * a + gb_ref[...].astype(F32) * b
    x = x_ref[...] + jnp.dot(merged.astype(BF16), wo_ref[...], preferred_element_type=F32)
    o_ref[...] = _ffn_step(x, g_ref, ffn_wi_ref, ffn_wo_ref, chunk)


def _merge_ffn(x, ya, yb, proj, wa, wb, wo, g, ffn_wi, ffn_wo, *, tm=512, chunk=256):
    n = x.shape[0]
    return pl.pallas_call(
        functools.partial(_merge_ffn_kernel, chunk=chunk),
        grid=(n // tm,),
        in_specs=[
            pl.BlockSpec((tm, D_MODEL), lambda i: (i, 0)),
            pl.BlockSpec((tm, NA_WIDTH), lambda i: (i, 0)),
            pl.BlockSpec((tm, NA_WIDTH), lambda i: (i, 0)),
            pl.BlockSpec((tm, D_MODEL), lambda i: (i, GA_OFF // D_MODEL)),
            pl.BlockSpec((tm, D_MODEL), lambda i: (i, GB_OFF // D_MODEL)),
            pl.BlockSpec((NA_WIDTH, D_MODEL), lambda i: (0, 0)),
            pl.BlockSpec((NA_WIDTH, D_MODEL), lambda i: (0, 0)),
            pl.BlockSpec((D_MODEL, D_MODEL), lambda i: (0, 0)),
            pl.BlockSpec((1, D_MODEL), lambda i: (0, 0)),
            pl.BlockSpec((D_MODEL, 2 * D_FF), lambda i: (0, 0)),
            pl.BlockSpec((D_FF, D_MODEL), lambda i: (0, 0)),
        ],
        out_specs=pl.BlockSpec((tm, D_MODEL), lambda i: (i, 0)),
        out_shape=jax.ShapeDtypeStruct((n, D_MODEL), F32),
        compiler_params=_params("parallel"),
        name="merge_ffn",
    )(x, ya, yb, proj, proj, wa, wb, wo, g, ffn_wi, ffn_wo)


def _rope_tables(seq_len):
    half = ROPE_DIM // 2
    inv_freq = jnp.power(ROPE_THETA, -jnp.arange(half, dtype=F32) * 2.0 / ROPE_DIM)
    ang = jnp.arange(seq_len, dtype=F32)[:, None] * inv_freq[None, :]
    cos, sin = jnp.cos(ang), jnp.sin(ang)
    ones = jnp.ones((seq_len, HEAD_DIM - ROPE_DIM), F32)
    zeros = jnp.zeros((seq_len, HEAD_DIM - ROPE_DIM), F32)
    zh = jnp.zeros_like(sin)
    c = jnp.concatenate([cos, cos, ones], axis=1)
    sa = jnp.concatenate([-sin, zh, zeros], axis=1)
    sb = jnp.concatenate([zh, sin, zeros], axis=1)
    reps = V7X_LANES // HEAD_DIM
    return tuple(jnp.tile(t, (1, reps)) for t in (c, sa, sb))


def _layer_consts(l, w):
    scale = HEAD_DIM ** -0.5
    heads = PROJ_BLOCK // HEAD_DIM
    head_gains = jnp.stack([
        jnp.tile(w["qa_norm"][l] * scale, heads), jnp.tile(w["ka_norm"][l], heads),
        jnp.tile(w["qb_norm"][l] * (scale * LOG2_E), heads), jnp.tile(w["kb_norm"][l], heads)])
    blk = np.arange(PROJ_BLOCK) // HEAD_DIM
    blockdiag = jnp.asarray((blk[:, None] == blk[None, :]).astype(np.float32) / HEAD_DIM, BF16)
    lam_init = 0.8 - 0.6 * math.exp(-0.3 * l)
    lam = (jnp.exp(jnp.sum(w["lam_q1"][l] * w["lam_k1"][l]))
           - jnp.exp(jnp.sum(w["lam_q2"][l] * w["lam_k2"][l])) + lam_init)
    score_bound = (HEAD_DIM * NORM_SLACK * jnp.max(jnp.abs(w["qb_norm"][l])) * (scale * LOG2_E)
                   * jnp.max(jnp.abs(w["kb_norm"][l])))
    return dict(
        score_bound=score_bound,
        ffn1_g=w["ffn1_norm"][l][None], ffn1_wi=w["ffn1_wi"][l].astype(BF16), ffn1_wo=w["ffn1_wo"][l].astype(BF16),
        mix_g=w["mix_norm"][l][None], w_in=w["w_in"][l].astype(BF16),
        head_gains=head_gains, blockdiag=blockdiag, bias=_natten_bias(w["rpb"][l]),
        scalars=jnp.stack([lam, jnp.asarray(1.0 - lam_init, F32)]).astype(F32),
        subln=w["subln"][l][:, None],
        w_a=w["w_a_out"][l].astype(BF16), w_b=w["w_b_out"][l].astype(BF16), w_o=w["w_o"][l].astype(BF16),
        ffn2_g=w["ffn2_norm"][l][None], ffn2_wi=w["ffn2_wi"][l].astype(BF16), ffn2_wo=w["ffn2_wo"][l].astype(BF16),
    )


def _encoder_layer(x, c, rope, batch, seq_len):
    x, proj, vt = _ffn_proj(x, c["ffn1_g"], c["ffn1_wi"], c["ffn1_wo"], c["mix_g"], c["w_in"], c["head_gains"],
                            c["blockdiag"], *rope, batch, seq_len, tk=_diff_key_chunk(seq_len))
    ya = _natten(proj, c["bias"], batch, seq_len)
    yb = lax.cond(
        c["score_bound"] <= DIFF_MAX_UNSTABILIZED_SCORE,
        lambda: _diffattn(proj, vt, c["scalars"], c["subln"], batch, seq_len, stabilize=False),
        lambda: _diffattn(proj, vt, c["scalars"], c["subln"], batch, seq_len, stabilize=True))
    return _merge_ffn(x, ya, yb, proj, c["w_a"], c["w_b"], c["w_o"], c["ffn2_g"], c["ffn2_wi"], c["ffn2_wo"])


def _encoder(groups, weights, depth):
    consts = [_layer_consts(l, weights) for l in range(depth)]
    outs = []
    for x in groups:
        batch, seq_len, _ = x.shape
        rope = _rope_tables(seq_len)
        y = x.reshape(batch * seq_len, D_MODEL)
        for c in consts:
            y = _encoder_layer(y, c, rope, batch, seq_len)
        outs.append(y.reshape(x.shape))
    return tuple(outs)


def kernel(x_prompt, x_sample, ffn1_norm, ffn1_wi, ffn1_wo, mix_norm, w_in, qa_norm, ka_norm, rpb, qb_norm, kb_norm, lam_q1, lam_k1, lam_q2, lam_k2, subln, w_a_out, w_b_out, w_o, ffn2_norm, ffn2_wi, ffn2_wo):
    weights = dict(ffn1_norm=ffn1_norm, ffn1_wi=ffn1_wi, ffn1_wo=ffn1_wo, mix_norm=mix_norm, w_in=w_in,
                   qa_norm=qa_norm, ka_norm=ka_norm, rpb=rpb, qb_norm=qb_norm, kb_norm=kb_norm,
                   lam_q1=lam_q1, lam_k1=lam_k1, lam_q2=lam_q2, lam_k2=lam_k2, subln=subln,
                   w_a_out=w_a_out, w_b_out=w_b_out, w_o=w_o,
                   ffn2_norm=ffn2_norm, ffn2_wi=ffn2_wi, ffn2_wo=ffn2_wo)
    return _encoder((x_prompt, x_sample), weights, ffn1_norm.shape[0])
```

```python
import functools
import math

import jax
import jax.numpy as jnp
import numpy as np
from jax import lax
from jax.experimental import pallas as pl
from jax.experimental.pallas import tpu as pltpu

F32 = jnp.float32
BF16 = jnp.bfloat16

D_MODEL = 1024
GRID_W = 64
HEAD_DIM = 64
NA_HEADS = 8
DIFF_HEADS = 4
NA_WIDTH = NA_HEADS * HEAD_DIM
DIFF_V_DIM = 2 * HEAD_DIM
IN_WIDTH = 5120
D_FF = 2816
WIN_H = 8
WIN_W = 16
ROPE_THETA = 500000.0
ROPE_DIM = HEAD_DIM // 4
NORM_EPS = 1e-6
SUBLN_EPS = 1e-5
NEG_INF = -1e30
FFN_RES = 0.5

V7X_VMEM_BYTES = 64 * 1024 * 1024
V7X_LANES = 128
VMEM_LIMIT = V7X_VMEM_BYTES * 7 // 8

PROJ_BLOCK = 512
W_QA, W_KA, W_VA, W_QB, W_KB, W_VB, W_GA, W_GB = 0, 512, 1024, 1536, 2048, 2560, 3072, 4096
GA_OFF, GB_OFF, QA_OFF, KA_OFF, VA_OFF, QB_OFF, KB_OFF = 0, 1024, 2048, 2560, 3072, 3584, 4096
PROJ_OUT = 4608
DIFF_TK_CHOICES = (1024, 512, 256)
DIFF_MIN_CHUNKS = 8
BF16_SUBLANES = 16
DIFF_VT_ROWS = DIFF_V_DIM + BF16_SUBLANES
LOG2_E = math.log2(math.e)
NORM_SLACK = 1.05
DIFF_MAX_UNSTABILIZED_SCORE = 32.0


def _params(*semantics):
    return pltpu.CompilerParams(dimension_semantics=semantics, vmem_limit_bytes=VMEM_LIMIT)


def _rms(x, g, eps):
    return x * lax.rsqrt(jnp.mean(x * x, axis=-1, keepdims=True) + eps) * g


def _ffn_step(x, g_ref, wi_ref, wo_ref, chunk):
    h = _rms(x, g_ref[...], NORM_EPS).astype(BF16)
    n_chunks = D_FF // chunk

    def up(c):
        a = jnp.dot(h, wi_ref[:, c * chunk:(c + 1) * chunk], preferred_element_type=F32)
        b = jnp.dot(h, wi_ref[:, D_FF + c * chunk:D_FF + (c + 1) * chunk], preferred_element_type=F32)
        return a, b

    acc = jnp.zeros(x.shape, F32)
    nxt = up(0)
    for c in range(n_chunks):
        a, b = nxt
        if c + 1 < n_chunks:
            nxt = up(c + 1)
        gate = (a * jax.nn.sigmoid(a) * b).astype(BF16)
        acc = acc + jnp.dot(gate, wo_ref[c * chunk:(c + 1) * chunk, :], preferred_element_type=F32)
    return x + FFN_RES * acc


def _ffn_proj_kernel(x_ref, ffn_g_ref, ffn_wi_ref, ffn_wo_ref, g_ref, w_ref, hg_ref, bd_ref,
                     cos_ref, sa_ref, sb_ref, x_out_ref, o_ref, vt_ref, *, chunk):
    x = _ffn_step(x_ref[...], ffn_g_ref, ffn_wi_ref, ffn_wo_ref, chunk)
    x_out_ref[...] = x
    h = _rms(x, g_ref[...], NORM_EPS).astype(BF16)
    tk = vt_ref.shape[-1]

    def proj(w_off):
        return jnp.dot(h, w_ref[:, w_off:w_off + PROJ_BLOCK], preferred_element_type=F32)

    def head_norm(p, gain_row):
        ms = jnp.dot((p * p).astype(BF16), bd_ref[...], preferred_element_type=F32)
        return p * lax.rsqrt(ms + NORM_EPS) * hg_ref[gain_row:gain_row + 1, :]

    reps = PROJ_BLOCK // V7X_LANES
    cos = jnp.concatenate([cos_ref[...]] * reps, axis=1)
    sa = jnp.concatenate([sa_ref[...]] * reps, axis=1)
    sb = jnp.concatenate([sb_ref[...]] * reps, axis=1)

    def rope(y):
        half = ROPE_DIM // 2
        up = pltpu.roll(y, PROJ_BLOCK - half, 1)
        dn = pltpu.roll(y, half, 1)
        return y * cos + up * sa + dn * sb

    def put(off, val):
        o_ref[:, off:off + PROJ_BLOCK] = val.astype(o_ref.dtype)

    def put_diff_values(vb):
        pad_row = lax.broadcasted_iota(jnp.int32, (BF16_SUBLANES, tk), 0)
        ones_rows = jnp.where(pad_row == 0, 1.0, 0.0).astype(vt_ref.dtype)
        for head in range(DIFF_HEADS):
            for c in range(vt_ref.shape[1]):
                blk = vb[c * tk:(c + 1) * tk, head * DIFF_V_DIM:(head + 1) * DIFF_V_DIM]
                vt_ref[head, c, :DIFF_V_DIM, :] = blk.T.astype(vt_ref.dtype)
                vt_ref[head, c, DIFF_V_DIM:, :] = ones_rows

    blocks = [
        (W_QA, lambda p: put(QA_OFF, head_norm(p, 0))),
        (W_GA, lambda p: put(GA_OFF, jax.nn.sigmoid(p))),
        (W_KA, lambda p: put(KA_OFF, head_norm(p, 1))),
        (W_GA + PROJ_BLOCK, lambda p: put(GA_OFF + PROJ_BLOCK, jax.nn.sigmoid(p))),
        (W_QB, lambda p: put(QB_OFF, rope(head_norm(p, 2)))),
        (W_GB, lambda p: put(GB_OFF, jax.nn.sigmoid(p))),
        (W_KB, lambda p: put(KB_OFF, rope(head_norm(p, 3)))),
        (W_GB + PROJ_BLOCK, lambda p: put(GB_OFF + PROJ_BLOCK, jax.nn.sigmoid(p))),
        (W_VB, put_diff_values),
        (W_VA, lambda p: put(VA_OFF, p)),
    ]
    nxt = proj(blocks[0][0])
    for n, (_, epilogue) in enumerate(blocks):
        p = nxt
        if n + 1 < len(blocks):
            nxt = proj(blocks[n + 1][0])
        epilogue(p)


def _diff_key_chunk(seq_len):
    fits = [t for t in DIFF_TK_CHOICES if seq_len % (2 * t) == 0]
    deep = [t for t in fits if seq_len // t >= DIFF_MIN_CHUNKS]
    return (deep or fits[-1:])[0]


def _ffn_proj(x, ffn_g, ffn_wi, ffn_wo, g, w, head_gains, blockdiag, cos, sa, sb, batch, seq_len, *,
              tk, tm=512, chunk=256):
    n = x.shape[0]
    tiles_per_seq = seq_len // tm
    if tm >= tk:
        vt_spec = pl.BlockSpec((None, DIFF_HEADS, tm // tk, DIFF_VT_ROWS, tk),
                               lambda i: (i // tiles_per_seq, 0, i % tiles_per_seq, 0, 0))
    else:
        sub = tk // tm
        vt_spec = pl.BlockSpec((None, DIFF_HEADS, 1, DIFF_VT_ROWS, tm),
                               lambda i: (i // tiles_per_seq, 0, (i % tiles_per_seq) // sub, 0,
                                          (i % tiles_per_seq) % sub))
    const = lambda i: (0, 0)
    return pl.pallas_call(
        functools.partial(_ffn_proj_kernel, chunk=chunk),
        grid=(n // tm,),
        in_specs=[
            pl.BlockSpec((tm, D_MODEL), lambda i: (i, 0)),
            pl.BlockSpec((1, D_MODEL), const),
            pl.BlockSpec((D_MODEL, 2 * D_FF), const),
            pl.BlockSpec((D_FF, D_MODEL), const),
            pl.BlockSpec((1, D_MODEL), const),
            pl.BlockSpec((D_MODEL, IN_WIDTH), const),
            pl.BlockSpec((4, PROJ_BLOCK), const),
            pl.BlockSpec((PROJ_BLOCK, PROJ_BLOCK), const),
            pl.BlockSpec((tm, V7X_LANES), lambda i: (i % tiles_per_seq, 0)),
            pl.BlockSpec((tm, V7X_LANES), lambda i: (i % tiles_per_seq, 0)),
            pl.BlockSpec((tm, V7X_LANES), lambda i: (i % tiles_per_seq, 0)),
        ],
        out_specs=[
            pl.BlockSpec((tm, D_MODEL), lambda i: (i, 0)),
            pl.BlockSpec((tm, PROJ_OUT), lambda i: (i, 0)),
            vt_spec,
        ],
        out_shape=[
            jax.ShapeDtypeStruct((n, D_MODEL), F32),
            jax.ShapeDtypeStruct((n, PROJ_OUT), BF16),
            jax.ShapeDtypeStruct((batch, DIFF_HEADS, seq_len // tk, DIFF_VT_ROWS, tk), BF16),
        ],
        compiler_params=_params("parallel"),
        name="ffn_proj",
    )(x, ffn_g, ffn_wi, ffn_wo, g, w, head_gains, blockdiag, cos, sa, sb)


def _natten_kernel(q_ref, k_ref, v_ref, bias_ref, o_ref, *, rows, rows_per_step, lookahead):
    n_keys = WIN_H * GRID_W
    r0 = pl.program_id(1) * rows_per_step
    block_start = jnp.clip(r0 - WIN_H // 2, 0, rows - (rows_per_step + WIN_H - 1))
    lane = lax.broadcasted_iota(jnp.int32, (GRID_W, V7X_LANES), 1)
    lo = lane < HEAD_DIM
    tasks = [(i, pair) for i in range(rows_per_step) for pair in range(NA_HEADS // 2)]

    def key_rows(i):
        start = jnp.clip(r0 + i - WIN_H // 2, 0, rows - WIN_H)
        variant = start - (r0 + i) + WIN_H - 1
        return pl.ds(pl.multiple_of((start - block_start) * GRID_W, GRID_W), n_keys), variant

    def scores(i, pair):
        keys, variant = key_rows(i)
        cols = slice(pair * V7X_LANES, (pair + 1) * V7X_LANES)
        qp = q_ref[i * GRID_W:(i + 1) * GRID_W, cols]
        zero = jnp.zeros_like(qp)
        lhs = jnp.concatenate([jnp.where(lo, qp, zero), jnp.where(lo, zero, qp)], axis=0)
        s = lax.dot_general(lhs, k_ref[keys, cols], (((1,), (1,)), ((), ())), preferred_element_type=F32)
        return s + bias_ref[variant, pair * 2 * GRID_W:(pair + 1) * 2 * GRID_W, :]

    pending = [scores(*task) for task in tasks[:lookahead]]
    for n, (i, pair) in enumerate(tasks):
        s = pending.pop(0)
        if n + lookahead < len(tasks):
            pending.append(scores(*tasks[n + lookahead]))
        keys, _ = key_rows(i)
        cols = slice(pair * V7X_LANES, (pair + 1) * V7X_LANES)
        m = jnp.max(s, axis=-1, keepdims=True)
        e = jnp.exp2(s - m)
        l = jnp.sum(e, axis=-1, keepdims=True)
        o = jnp.dot(e.astype(BF16), v_ref[keys, cols], preferred_element_type=F32)
        o = o * (1.0 / l)
        o_ref[i * GRID_W:(i + 1) * GRID_W, cols] = jnp.where(lo, o[:GRID_W], o[GRID_W:]).astype(o_ref.dtype)


def _natten(proj, bias, batch, seq_len, *, rows_per_step=8, lookahead=3):
    rows = seq_len // GRID_W
    n = proj.shape[0]
    block_rows = rows_per_step + WIN_H - 1
    assert rows % rows_per_step == 0 and rows >= block_rows
    steps = rows // rows_per_step

    def block_start(j):
        return jnp.clip(j * rows_per_step - WIN_H // 2, 0, rows - block_rows)

    return pl.pallas_call(
        functools.partial(_natten_kernel, rows=rows, rows_per_step=rows_per_step, lookahead=lookahead),
        grid=(batch, steps),
        in_specs=[
            pl.BlockSpec((rows_per_step * GRID_W, NA_WIDTH), lambda b, j: (b * steps + j, QA_OFF // NA_WIDTH)),
            pl.BlockSpec((pl.Element(block_rows * GRID_W), pl.Element(NA_WIDTH)),
                         lambda b, j: ((b * rows + block_start(j)) * GRID_W, KA_OFF)),
            pl.BlockSpec((pl.Element(block_rows * GRID_W), pl.Element(NA_WIDTH)),
                         lambda b, j: ((b * rows + block_start(j)) * GRID_W, VA_OFF)),
            pl.BlockSpec((WIN_H, NA_HEADS * GRID_W, WIN_H * GRID_W), lambda b, j: (0, 0, 0)),
        ],
        out_specs=pl.BlockSpec((rows_per_step * GRID_W, NA_WIDTH), lambda b, j: (b * steps + j, 0)),
        out_shape=jax.ShapeDtypeStruct((n, NA_WIDTH), BF16),
        compiler_params=_params("parallel", "arbitrary"),
        name="natten",
    )(proj, proj, proj, bias)


def _natten_bias(rpb):
    qc = np.arange(GRID_W)[:, None]
    kc = np.arange(GRID_W)[None, :]
    wstart = np.clip(qc - WIN_W // 2, 0, GRID_W - WIN_W)
    in_win = (kc >= wstart) & (kc < wstart + WIN_W)
    dc = np.clip(kc - qc + WIN_W - 1, 0, 2 * WIN_W - 2)
    dr = np.arange(WIN_H)[:, None] + np.arange(WIN_H)[None, :]
    pick_dc = (dc[..., None] == np.arange(2 * WIN_W - 1)).astype(np.float32)
    pick_dr = (dr[..., None] == np.arange(2 * WIN_H - 1)).astype(np.float32)
    b = jnp.einsum("tsr,hrd,qkd->thqsk", pick_dr, rpb.astype(F32), pick_dc,
                   precision=lax.Precision.HIGHEST)
    b = jnp.where(in_win[None, None, :, None, :], b, NEG_INF)
    return b.reshape(WIN_H, NA_HEADS * GRID_W, WIN_H * GRID_W)


def _diff_kernel(sc_ref, q_ref, k_ref, vt_ref, g_ref, o_ref,
                 qst_scr, s_a, s_b, cm_a, cm_b, p_a, p_b, al_a, al_b, m_scr, acc_scr, *, tk, heads, stabilize):
    tq = q_ref.shape[0]
    n_chunks = k_ref.shape[0] // tk
    bufs = ((s_a, cm_a, p_a, al_a), (s_b, cm_b, p_b, al_b))

    def head_cols(h):
        return slice(h * DIFF_V_DIM, (h + 1) * DIFF_V_DIM)

    def prepare(h):
        qt = q_ref[:, head_cols(h)].astype(F32).T
        row = lax.broadcasted_iota(jnp.int32, qt.shape, 0)
        zero = jnp.zeros_like(qt)
        qst_scr[h] = jnp.concatenate(
            [jnp.where(row < HEAD_DIM, qt, zero), jnp.where(row < HEAD_DIM, zero, qt)], axis=1).astype(BF16)
        m_scr[h] = jnp.full(m_scr.shape[1:], NEG_INF if stabilize else 0.0, F32)
        acc_scr[h] = jnp.zeros(acc_scr.shape[1:], F32)

    def scores(h, c, buf):
        s_ref, cm_ref, p_ref, _ = buf
        start = pl.multiple_of(c * tk, tk)
        s = jnp.dot(k_ref[pl.ds(start, tk), head_cols(h)], qst_scr[h], preferred_element_type=F32)
        if stabilize:
            s_ref[...] = s
            cm_ref[...] = jnp.max(s, axis=0, keepdims=True)
        else:
            p = jnp.exp2(s)
            p_ref[...] = p.astype(BF16)
            m_scr[h] = m_scr[h] + jnp.sum(p, axis=0, keepdims=True)

    def softmax(h, buf):
        s_ref, cm_ref, p_ref, al_ref = buf
        m_prev = m_scr[h]
        m_new = jnp.maximum(m_prev, cm_ref[...])
        al_ref[...] = jnp.exp2(m_prev - m_new)
        m_scr[h] = m_new
        p_ref[...] = jnp.exp2(s_ref[...] - m_new).astype(BF16)

    def accumulate(h, c, buf):
        _, _, p_ref, al_ref = buf
        if stabilize:
            pv = jnp.dot(vt_ref[h, c], p_ref[...], preferred_element_type=F32)
            acc_scr[h] = al_ref[...] * acc_scr[h] + pv
        else:
            pv = jnp.dot(vt_ref[h, c, :DIFF_V_DIM, :], p_ref[...], preferred_element_type=F32)
            acc_scr[h, :DIFF_V_DIM, :] = acc_scr[h, :DIFF_V_DIM, :] + pv

    def finalize(h):
        acc = acc_scr[h]
        total = acc[DIFF_V_DIM:DIFF_V_DIM + 1] if stabilize else m_scr[h]
        o = acc[:DIFF_V_DIM] * (1.0 / total)
        d = o[:, :tq] - sc_ref[0] * o[:, tq:]
        ms = jnp.mean(d * d, axis=0, keepdims=True)
        y = d * lax.rsqrt(ms + SUBLN_EPS) * (g_ref[...] * sc_ref[1])
        o_ref[:, head_cols(h)] = y.T.astype(o_ref.dtype)

    def turn(h, c, parity, nxt, prev):
        if nxt is not None:
            if nxt[0] != h:
                prepare(nxt[0])
            scores(nxt[0], nxt[1], bufs[1 - parity])
        if not stabilize:
            accumulate(h, c, bufs[parity])
            if nxt is None or nxt[0] != h:
                finalize(h)
            return
        softmax(h, bufs[parity])
        if prev is not None:
            accumulate(prev[0], prev[1], bufs[1 - parity])
            if prev[0] != h:
                finalize(prev[0])

    prepare(0)
    scores(0, 0, bufs[0])
    if heads == 1 and n_chunks > 4 and (n_chunks - 4) % 4 == 0:
        turn(0, 0, 0, (0, 1), None)
        turn(0, 1, 1, (0, 2), (0, 0))

        def body(i, carry):
            for j in range(4):
                c = 2 + 4 * i + j
                turn(0, c, j % 2, (0, c + 1), (0, c - 1))
            return carry
        lax.fori_loop(0, (n_chunks - 4) // 4, body, 0)
        turn(0, n_chunks - 2, 0, (0, n_chunks - 1), (0, n_chunks - 3))
        turn(0, n_chunks - 1, 1, None, (0, n_chunks - 2))
    else:
        tasks = [(h, c) for h in range(heads) for c in range(n_chunks)]
        for t, (h, c) in enumerate(tasks):
            turn(h, c, t % 2, tasks[t + 1] if t + 1 < len(tasks) else None, tasks[t - 1] if t else None)
    if stabilize:
        accumulate(heads - 1, n_chunks - 1, bufs[1])
        finalize(heads - 1)


def _diffattn(proj, vt, scalars, subln_col, batch, seq_len, *, stabilize, tq=512, max_unrolled_tasks=32):
    n = proj.shape[0]
    tk = vt.shape[-1]
    n_chunks = seq_len // tk
    q_tiles = seq_len // tq
    assert n_chunks % 2 == 0
    heads = DIFF_HEADS if DIFF_HEADS * n_chunks <= max_unrolled_tasks else 1
    width = heads * DIFF_V_DIM
    return pl.pallas_call(
        functools.partial(_diff_kernel, tk=tk, heads=heads, stabilize=stabilize),
        grid=(batch, DIFF_HEADS // heads, q_tiles),
        in_specs=[
            pl.BlockSpec(memory_space=pltpu.SMEM),
            pl.BlockSpec((tq, width), lambda b, h, i: (b * q_tiles + i, QB_OFF // width + h)),
            pl.BlockSpec((seq_len, width), lambda b, h, i: (b, KB_OFF // width + h)),
            pl.BlockSpec((None, heads, n_chunks, DIFF_VT_ROWS, tk), lambda b, h, i: (b, h, 0, 0, 0)),
            pl.BlockSpec((DIFF_V_DIM, 1), lambda b, h, i: (0, 0)),
        ],
        out_specs=pl.BlockSpec((tq, width), lambda b, h, i: (b * q_tiles + i, h)),
        out_shape=jax.ShapeDtypeStruct((n, DIFF_HEADS * DIFF_V_DIM), BF16),
        scratch_shapes=[
            pltpu.VMEM((heads, DIFF_V_DIM, 2 * tq), BF16),
            pltpu.VMEM((tk, 2 * tq), F32),
            pltpu.VMEM((tk, 2 * tq), F32),
            pltpu.VMEM((1, 2 * tq), F32),
            pltpu.VMEM((1, 2 * tq), F32),
            pltpu.VMEM((tk, 2 * tq), BF16),
            pltpu.VMEM((tk, 2 * tq), BF16),
            pltpu.VMEM((1, 2 * tq), F32),
            pltpu.VMEM((1, 2 * tq), F32),
            pltpu.VMEM((heads, 1, 2 * tq), F32),
            pltpu.VMEM((heads, DIFF_VT_ROWS, 2 * tq), F32),
        ],
        compiler_params=_params("parallel", "parallel", "arbitrary"),
        name="diffattn",
    )(scalars, proj, proj, vt, subln_col)


def _merge_ffn_kernel(x_ref, ya_ref, yb_ref, ga_ref, gb_ref, wa_ref, wb_ref, wo_ref,
                      g_ref, ffn_wi_ref, ffn_wo_ref, o_ref, *, chunk):
    a = jnp.dot(ya_ref[...], wa_ref[...], preferred_element_type=F32)
    b = jnp.dot(yb_ref[...], wb_ref[...], preferred_element_type=F32)
    merged = ga_ref[...].astype(F32) * a + gb_ref[...].astype(F32) * b
    x = x_ref[...] + jnp.dot(merged.astype(BF16), wo_ref[...], preferred_element_type=F32)
    o_ref[...] = _ffn_step(x, g_ref, ffn_wi_ref, ffn_wo_ref, chunk)


def _merge_ffn(x, ya, yb, proj, wa, wb, wo, g, ffn_wi, ffn_wo, *, tm=512, chunk=256):
    n = x.shape[0]
    return pl.pallas_call(
        functools.partial(_merge_ffn_kernel, chunk=chunk),
        grid=(n // tm,),
        in_specs=[
            pl.BlockSpec((tm, D_MODEL), lambda i: (i, 0)),
            pl.BlockSpec((tm, NA_WIDTH), lambda i: (i, 0)),
            pl.BlockSpec((tm, NA_WIDTH), lambda i: (i, 0)),
            pl.BlockSpec((tm, D_MODEL), lambda i: (i, GA_OFF // D_MODEL)),
            pl.BlockSpec((tm, D_MODEL), lambda i: (i, GB_OFF // D_MODEL)),
            pl.BlockSpec((NA_WIDTH, D_MODEL), lambda i: (0, 0)),
            pl.BlockSpec((NA_WIDTH, D_MODEL), lambda i: (0, 0)),
            pl.BlockSpec((D_MODEL, D_MODEL), lambda i: (0, 0)),
            pl.BlockSpec((1, D_MODEL), lambda i: (0, 0)),
            pl.BlockSpec((D_MODEL, 2 * D_FF), lambda i: (0, 0)),
            pl.BlockSpec((D_FF, D_MODEL), lambda i: (0, 0)),
        ],
        out_specs=pl.BlockSpec((tm, D_MODEL), lambda i: (i, 0)),
        out_shape=jax.ShapeDtypeStruct((n, D_MODEL), F32),
        compiler_params=_params("parallel"),
        name="merge_ffn",
    )(x, ya, yb, proj, proj, wa, wb, wo, g, ffn_wi, ffn_wo)


def _rope_tables(seq_len):
    half = ROPE_DIM // 2
    inv_freq = jnp.power(ROPE_THETA, -jnp.arange(half, dtype=F32) * 2.0 / ROPE_DIM)
    ang = jnp.arange(seq_len, dtype=F32)[:, None] * inv_freq[None, :]
    cos, sin = jnp.cos(ang), jnp.sin(ang)
    ones = jnp.ones((seq_len, HEAD_DIM - ROPE_DIM), F32)
    zeros = jnp.zeros((seq_len, HEAD_DIM - ROPE_DIM), F32)
    zh = jnp.zeros_like(sin)
    c = jnp.concatenate([cos, cos, ones], axis=1)
    sa = jnp.concatenate([-sin, zh, zeros], axis=1)
    sb = jnp.concatenate([zh, sin, zeros], axis=1)
    reps = V7X_LANES // HEAD_DIM
    return tuple(jnp.tile(t, (1, reps)) for t in (c, sa, sb))


def _layer_consts(l, w):
    scale = HEAD_DIM ** -0.5
    heads = PROJ_BLOCK // HEAD_DIM
    head_gains = jnp.stack([
        jnp.tile(w["qa_norm"][l] * (scale * LOG2_E), heads), jnp.tile(w["ka_norm"][l], heads),
        jnp.tile(w["qb_norm"][l] * (scale * LOG2_E), heads), jnp.tile(w["kb_norm"][l], heads)])
    blk = np.arange(PROJ_BLOCK) // HEAD_DIM
    blockdiag = jnp.asarray((blk[:, None] == blk[None, :]).astype(np.float32) / HEAD_DIM, BF16)
    lam_init = 0.8 - 0.6 * math.exp(-0.3 * l)
    lam = (jnp.exp(jnp.sum(w["lam_q1"][l] * w["lam_k1"][l]))
           - jnp.exp(jnp.sum(w["lam_q2"][l] * w["lam_k2"][l])) + lam_init)
    score_bound = (HEAD_DIM * NORM_SLACK * jnp.max(jnp.abs(w["qb_norm"][l])) * (scale * LOG2_E)
                   * jnp.max(jnp.abs(w["kb_norm"][l])))
    return dict(
        score_bound=score_bound,
        ffn1_g=w["ffn1_norm"][l][None], ffn1_wi=w["ffn1_wi"][l].astype(BF16), ffn1_wo=w["ffn1_wo"][l].astype(BF16),
        mix_g=w["mix_norm"][l][None], w_in=w["w_in"][l].astype(BF16),
        head_gains=head_gains, blockdiag=blockdiag, bias=_natten_bias(w["rpb"][l] * LOG2_E),
        scalars=jnp.stack([lam, jnp.asarray(1.0 - lam_init, F32)]).astype(F32),
        subln=w["subln"][l][:, None],
        w_a=w["w_a_out"][l].astype(BF16), w_b=w["w_b_out"][l].astype(BF16), w_o=w["w_o"][l].astype(BF16),
        ffn2_g=w["ffn2_norm"][l][None], ffn2_wi=w["ffn2_wi"][l].astype(BF16), ffn2_wo=w["ffn2_wo"][l].astype(BF16),
    )


def _encoder_layer(x, c, rope, batch, seq_len):
    x, proj, vt = _ffn_proj(x, c["ffn1_g"], c["ffn1_wi"], c["ffn1_wo"], c["mix_g"], c["w_in"], c["head_gains"],
                            c["blockdiag"], *rope, batch, seq_len, tk=_diff_key_chunk(seq_len))
    ya = _natten(proj, c["bias"], batch, seq_len)
    yb = lax.cond(
        c["score_bound"] <= DIFF_MAX_UNSTABILIZED_SCORE,
        lambda: _diffattn(proj, vt, c["scalars"], c["subln"], batch, seq_len, stabilize=False),
        lambda: _diffattn(proj, vt, c["scalars"], c["subln"], batch, seq_len, stabilize=True))
    return _merge_ffn(x, ya, yb, proj, c["w_a"], c["w_b"], c["w_o"], c["ffn2_g"], c["ffn2_wi"], c["ffn2_wo"])


def _encoder(groups, weights, depth):
    consts = [_layer_consts(l, weights) for l in range(depth)]
    outs = []
    for x in groups:
        batch, seq_len, _ = x.shape
        rope = _rope_tables(seq_len)
        y = x.reshape(batch * seq_len, D_MODEL)
        for c in consts:
            y = _encoder_layer(y, c, rope, batch, seq_len)
        outs.append(y.reshape(x.shape))
    return tuple(outs)


def kernel(x_prompt, x_sample, ffn1_norm, ffn1_wi, ffn1_wo, mix_norm, w_in, qa_norm, ka_norm, rpb, qb_norm, kb_norm, lam_q1, lam_k1, lam_q2, lam_k2, subln, w_a_out, w_b_out, w_o, ffn2_norm, ffn2_wi, ffn2_wo):
    weights = dict(ffn1_norm=ffn1_norm, ffn1_wi=ffn1_wi, ffn1_wo=ffn1_wo, mix_norm=mix_norm, w_in=w_in,
                   qa_norm=qa_norm, ka_norm=ka_norm, rpb=rpb, qb_norm=qb_norm, kb_norm=kb_norm,
                   lam_q1=lam_q1, lam_k1=lam_k1, lam_q2=lam_q2, lam_k2=lam_k2, subln=subln,
                   w_a_out=w_a_out, w_b_out=w_b_out, w_o=w_o,
                   ffn2_norm=ffn2_norm, ffn2_wi=ffn2_wi, ffn2_wo=ffn2_wo)
    return _encoder((x_prompt, x_sample), weights, ffn1_norm.shape[0])
```

```python
import functools
import math

import jax
import jax.numpy as jnp
import numpy as np
from jax import lax
from jax.experimental import pallas as pl
from jax.experimental.pallas import tpu as pltpu

F32 = jnp.float32
BF16 = jnp.bfloat16

D_MODEL = 1024
GRID_W = 64
HEAD_DIM = 64
NA_HEADS = 8
DIFF_HEADS = 4
NA_WIDTH = NA_HEADS * HEAD_DIM
DIFF_V_DIM = 2 * HEAD_DIM
IN_WIDTH = 5120
D_FF = 2816
WIN_H = 8
WIN_W = 16
ROPE_THETA = 500000.0
ROPE_DIM = HEAD_DIM // 4
NORM_EPS = 1e-6
SUBLN_EPS = 1e-5
NEG_INF = -1e30
FFN_RES = 0.5

V7X_VMEM_BYTES = 64 * 1024 * 1024
V7X_LANES = 128
VMEM_LIMIT = V7X_VMEM_BYTES * 7 // 8

PROJ_BLOCK = 512
W_QA, W_KA, W_VA, W_QB, W_KB, W_VB, W_GA, W_GB = 0, 512, 1024, 1536, 2048, 2560, 3072, 4096
GA_OFF, GB_OFF, QA_OFF, KA_OFF, VA_OFF, QB_OFF, KB_OFF = 0, 1024, 2048, 2560, 3072, 3584, 4096
PROJ_OUT = 4608
DIFF_TK_CHOICES = (1024, 512, 256)
DIFF_MIN_CHUNKS = 8
BF16_SUBLANES = 16
DIFF_VT_ROWS = DIFF_V_DIM + BF16_SUBLANES
LOG2_E = math.log2(math.e)
NORM_SLACK = 1.05
DIFF_MAX_UNSTABILIZED_SCORE = 32.0


def _params(*semantics):
    return pltpu.CompilerParams(dimension_semantics=semantics, vmem_limit_bytes=VMEM_LIMIT)


def _rms(x, g, eps):
    return x * lax.rsqrt(jnp.mean(x * x, axis=-1, keepdims=True) + eps) * g


def _interleave(*stage_generators):
    live = list(stage_generators)
    while live:
        for gen in list(live):
            try:
                next(gen)
            except StopIteration:
                live.remove(gen)


def _ffn_stages(x, g_ref, wi_ref, wo_ref, chunk):
    h = _rms(x, g_ref[...], NORM_EPS).astype(BF16)
    n_chunks = D_FF // chunk

    def up(c):
        a = jnp.dot(h, wi_ref[:, c * chunk:(c + 1) * chunk], preferred_element_type=F32)
        b = jnp.dot(h, wi_ref[:, D_FF + c * chunk:D_FF + (c + 1) * chunk], preferred_element_type=F32)
        return a, b

    acc = jnp.zeros(x.shape, F32)
    nxt = up(0)
    yield
    for c in range(n_chunks):
        a, b = nxt
        if c + 1 < n_chunks:
            nxt = up(c + 1)
        gate = (a * jax.nn.sigmoid(a) * b).astype(BF16)
        acc = acc + jnp.dot(gate, wo_ref[c * chunk:(c + 1) * chunk, :], preferred_element_type=F32)
        yield
    return x + FFN_RES * acc


def _ffn_proj_kernel(x_ref, ffn_g_ref, ffn_wi_ref, ffn_wo_ref, g_ref, w_ref, hg_ref, bd_ref,
                     cos_ref, sa_ref, sb_ref, x_out_ref, o_ref, vt_ref, *, chunk):
    tile_rows = x_ref.shape[0]
    n_slabs, slab = vt_ref.shape[1], vt_ref.shape[-1]

    def stages(r0, nrows):
        rows = slice(r0, r0 + nrows)
        x = yield from _ffn_stages(x_ref[rows, :], ffn_g_ref, ffn_wi_ref, ffn_wo_ref, chunk)
        x_out_ref[rows, :] = x
        h = _rms(x, g_ref[...], NORM_EPS).astype(BF16)

        def proj(w_off):
            return jnp.dot(h, w_ref[:, w_off:w_off + PROJ_BLOCK], preferred_element_type=F32)

        def head_norm(p, gain_row):
            ms = jnp.dot((p * p).astype(BF16), bd_ref[...], preferred_element_type=F32)
            return p * lax.rsqrt(ms + NORM_EPS) * hg_ref[gain_row:gain_row + 1, :]

        reps = PROJ_BLOCK // V7X_LANES
        cos = jnp.concatenate([cos_ref[rows, :]] * reps, axis=1)
        sa = jnp.concatenate([sa_ref[rows, :]] * reps, axis=1)
        sb = jnp.concatenate([sb_ref[rows, :]] * reps, axis=1)

        def rope(y):
            half = ROPE_DIM // 2
            up = pltpu.roll(y, PROJ_BLOCK - half, 1)
            dn = pltpu.roll(y, half, 1)
            return y * cos + up * sa + dn * sb

        def put(off, val):
            o_ref[rows, off:off + PROJ_BLOCK] = val.astype(o_ref.dtype)

        def put_diff_values(vb):
            width = min(nrows, slab)
            pad_row = lax.broadcasted_iota(jnp.int32, (BF16_SUBLANES, width), 0)
            ones_rows = jnp.where(pad_row == 0, 1.0, 0.0).astype(vt_ref.dtype)
            for head in range(DIFF_HEADS):
                for start in range(0, nrows, width):
                    c, lane0 = (r0 + start) // slab, (r0 + start) % slab
                    blk = vb[start:start + width, head * DIFF_V_DIM:(head + 1) * DIFF_V_DIM]
                    vt_ref[head, c, :DIFF_V_DIM, lane0:lane0 + width] = blk.T.astype(vt_ref.dtype)
                    vt_ref[head, c, DIFF_V_DIM:, lane0:lane0 + width] = ones_rows

        blocks = [
            (W_QA, lambda p: put(QA_OFF, head_norm(p, 0))),
            (W_GA, lambda p: put(GA_OFF, jax.nn.sigmoid(p))),
            (W_KA, lambda p: put(KA_OFF, head_norm(p, 1))),
            (W_GA + PROJ_BLOCK, lambda p: put(GA_OFF + PROJ_BLOCK, jax.nn.sigmoid(p))),
            (W_QB, lambda p: put(QB_OFF, rope(head_norm(p, 2)))),
            (W_GB, lambda p: put(GB_OFF, jax.nn.sigmoid(p))),
            (W_KB, lambda p: put(KB_OFF, rope(head_norm(p, 3)))),
            (W_GB + PROJ_BLOCK, lambda p: put(GB_OFF + PROJ_BLOCK, jax.nn.sigmoid(p))),
            (W_VB, put_diff_values),
            (W_VA, lambda p: put(VA_OFF, p)),
        ]
        nxt = proj(blocks[0][0])
        yield
        for n, (_, epilogue) in enumerate(blocks):
            p = nxt
            if n + 1 < len(blocks):
                nxt = proj(blocks[n + 1][0])
            epilogue(p)
            yield

    assert n_slabs * slab == tile_rows
    half = tile_rows // 2
    _interleave(stages(0, half), stages(half, half))


def _diff_key_chunk(seq_len):
    fits = [t for t in DIFF_TK_CHOICES if seq_len % (2 * t) == 0]
    deep = [t for t in fits if seq_len // t >= DIFF_MIN_CHUNKS]
    return (deep or fits[-1:])[0]


def _ffn_proj(x, ffn_g, ffn_wi, ffn_wo, g, w, head_gains, blockdiag, cos, sa, sb, batch, seq_len, *,
              tk, tm=512, chunk=256):
    n = x.shape[0]
    tiles_per_seq = seq_len // tm
    if tm >= tk:
        vt_spec = pl.BlockSpec((None, DIFF_HEADS, tm // tk, DIFF_VT_ROWS, tk),
                               lambda i: (i // tiles_per_seq, 0, i % tiles_per_seq, 0, 0))
    else:
        sub = tk // tm
        vt_spec = pl.BlockSpec((None, DIFF_HEADS, 1, DIFF_VT_ROWS, tm),
                               lambda i: (i // tiles_per_seq, 0, (i % tiles_per_seq) // sub, 0,
                                          (i % tiles_per_seq) % sub))
    const = lambda i: (0, 0)
    return pl.pallas_call(
        functools.partial(_ffn_proj_kernel, chunk=chunk),
        grid=(n // tm,),
        in_specs=[
            pl.BlockSpec((tm, D_MODEL), lambda i: (i, 0)),
            pl.BlockSpec((1, D_MODEL), const),
            pl.BlockSpec((D_MODEL, 2 * D_FF), const),
            pl.BlockSpec((D_FF, D_MODEL), const),
            pl.BlockSpec((1, D_MODEL), const),
            pl.BlockSpec((D_MODEL, IN_WIDTH), const),
            pl.BlockSpec((4, PROJ_BLOCK), const),
            pl.BlockSpec((PROJ_BLOCK, PROJ_BLOCK), const),
            pl.BlockSpec((tm, V7X_LANES), lambda i: (i % tiles_per_seq, 0)),
            pl.BlockSpec((tm, V7X_LANES), lambda i: (i % tiles_per_seq, 0)),
            pl.BlockSpec((tm, V7X_LANES), lambda i: (i % tiles_per_seq, 0)),
        ],
        out_specs=[
            pl.BlockSpec((tm, D_MODEL), lambda i: (i, 0)),
            pl.BlockSpec((tm, PROJ_OUT), lambda i: (i, 0)),
            vt_spec,
        ],
        out_shape=[
            jax.ShapeDtypeStruct((n, D_MODEL), F32),
            jax.ShapeDtypeStruct((n, PROJ_OUT), BF16),
            jax.ShapeDtypeStruct((batch, DIFF_HEADS, seq_len // tk, DIFF_VT_ROWS, tk), BF16),
        ],
        compiler_params=_params("parallel"),
        name="ffn_proj",
    )(x, ffn_g, ffn_wi, ffn_wo, g, w, head_gains, blockdiag, cos, sa, sb)


def _natten_kernel(q_ref, k_ref, v_ref, bias_ref, o_ref, *, rows, rows_per_step, lookahead):
    n_keys = WIN_H * GRID_W
    r0 = pl.program_id(1) * rows_per_step
    block_start = jnp.clip(r0 - WIN_H // 2, 0, rows - (rows_per_step + WIN_H - 1))
    lane = lax.broadcasted_iota(jnp.int32, (GRID_W, V7X_LANES), 1)
    lo = lane < HEAD_DIM
    tasks = [(i, pair) for i in range(rows_per_step) for pair in range(NA_HEADS // 2)]

    def key_rows(i):
        start = jnp.clip(r0 + i - WIN_H // 2, 0, rows - WIN_H)
        variant = start - (r0 + i) + WIN_H - 1
        return pl.ds(pl.multiple_of((start - block_start) * GRID_W, GRID_W), n_keys), variant

    def scores(i, pair):
        keys, variant = key_rows(i)
        cols = slice(pair * V7X_LANES, (pair + 1) * V7X_LANES)
        qp = q_ref[i * GRID_W:(i + 1) * GRID_W, cols]
        zero = jnp.zeros_like(qp)
        lhs = jnp.concatenate([jnp.where(lo, qp, zero), jnp.where(lo, zero, qp)], axis=0)
        s = lax.dot_general(lhs, k_ref[keys, cols], (((1,), (1,)), ((), ())), preferred_element_type=F32)
        return s + bias_ref[variant, pair * 2 * GRID_W:(pair + 1) * 2 * GRID_W, :]

    pending = [scores(*task) for task in tasks[:lookahead]]
    for n, (i, pair) in enumerate(tasks):
        s = pending.pop(0)
        if n + lookahead < len(tasks):
            pending.append(scores(*tasks[n + lookahead]))
        keys, _ = key_rows(i)
        cols = slice(pair * V7X_LANES, (pair + 1) * V7X_LANES)
        m = jnp.max(s, axis=-1, keepdims=True)
        e = jnp.exp2(s - m)
        l = jnp.sum(e, axis=-1, keepdims=True)
        o = jnp.dot(e.astype(BF16), v_ref[keys, cols], preferred_element_type=F32)
        o = o * (1.0 / l)
        o_ref[i * GRID_W:(i + 1) * GRID_W, cols] = jnp.where(lo, o[:GRID_W], o[GRID_W:]).astype(o_ref.dtype)


def _natten(proj, bias, batch, seq_len, *, rows_per_step=8, lookahead=3):
    rows = seq_len // GRID_W
    n = proj.shape[0]
    block_rows = rows_per_step + WIN_H - 1
    assert rows % rows_per_step == 0 and rows >= block_rows
    steps = rows // rows_per_step

    def block_start(j):
        return jnp.clip(j * rows_per_step - WIN_H // 2, 0, rows - block_rows)

    return pl.pallas_call(
        functools.partial(_natten_kernel, rows=rows, rows_per_step=rows_per_step, lookahead=lookahead),
        grid=(batch, steps),
        in_specs=[
            pl.BlockSpec((rows_per_step * GRID_W, NA_WIDTH), lambda b, j: (b * steps + j, QA_OFF // NA_WIDTH)),
            pl.BlockSpec((pl.Element(block_rows * GRID_W), pl.Element(NA_WIDTH)),
                         lambda b, j: ((b * rows + block_start(j)) * GRID_W, KA_OFF)),
            pl.BlockSpec((pl.Element(block_rows * GRID_W), pl.Element(NA_WIDTH)),
                         lambda b, j: ((b * rows + block_start(j)) * GRID_W, VA_OFF)),
            pl.BlockSpec((WIN_H, NA_HEADS * GRID_W, WIN_H * GRID_W), lambda b, j: (0, 0, 0)),
        ],
        out_specs=pl.BlockSpec((rows_per_step * GRID_W, NA_WIDTH), lambda b, j: (b * steps + j, 0)),
        out_shape=jax.ShapeDtypeStruct((n, NA_WIDTH), BF16),
        compiler_params=_params("parallel", "arbitrary"),
        name="natten",
    )(proj, proj, proj, bias)


def _natten_bias(rpb):
    qc = np.arange(GRID_W)[:, None]
    kc = np.arange(GRID_W)[None, :]
    wstart = np.clip(qc - WIN_W // 2, 0, GRID_W - WIN_W)
    in_win = (kc >= wstart) & (kc < wstart + WIN_W)
    dc = np.clip(kc - qc + WIN_W - 1, 0, 2 * WIN_W - 2)
    dr = np.arange(WIN_H)[:, None] + np.arange(WIN_H)[None, :]
    pick_dc = (dc[..., None] == np.arange(2 * WIN_W - 1)).astype(np.float32)
    pick_dr = (dr[..., None] == np.arange(2 * WIN_H - 1)).astype(np.float32)
    b = jnp.einsum("tsr,hrd,qkd->thqsk", pick_dr, rpb.astype(F32), pick_dc,
                   precision=lax.Precision.HIGHEST)
    b = jnp.where(in_win[None, None, :, None, :], b, NEG_INF)
    return b.reshape(WIN_H, NA_HEADS * GRID_W, WIN_H * GRID_W)


def _diff_kernel(sc_ref, q_ref, k_ref, vt_ref, g_ref, o_ref,
                 qst_scr, s_a, s_b, cm_a, cm_b, p_a, p_b, al_a, al_b, m_scr, acc_scr, *, tk, heads, stabilize):
    tq = q_ref.shape[0]
    n_chunks = k_ref.shape[0] // tk
    bufs = ((s_a, cm_a, p_a, al_a), (s_b, cm_b, p_b, al_b))

    def head_cols(h):
        return slice(h * DIFF_V_DIM, (h + 1) * DIFF_V_DIM)

    def prepare(h):
        qt = q_ref[:, head_cols(h)].astype(F32).T
        row = lax.broadcasted_iota(jnp.int32, qt.shape, 0)
        zero = jnp.zeros_like(qt)
        qst_scr[h] = jnp.concatenate(
            [jnp.where(row < HEAD_DIM, qt, zero), jnp.where(row < HEAD_DIM, zero, qt)], axis=1).astype(BF16)
        m_scr[h] = jnp.full(m_scr.shape[1:], NEG_INF if stabilize else 0.0, F32)
        acc_scr[h] = jnp.zeros(acc_scr.shape[1:], F32)

    def scores(h, c, buf):
        s_ref, cm_ref, p_ref, _ = buf
        start = pl.multiple_of(c * tk, tk)
        s = jnp.dot(k_ref[pl.ds(start, tk), head_cols(h)], qst_scr[h], preferred_element_type=F32)
        if stabilize:
            s_ref[...] = s
            cm_ref[...] = jnp.max(s, axis=0, keepdims=True)
        else:
            p = jnp.exp2(s)
            p_ref[...] = p.astype(BF16)
            m_scr[h] = m_scr[h] + jnp.sum(p, axis=0, keepdims=True)

    def softmax(h, buf):
        s_ref, cm_ref, p_ref, al_ref = buf
        m_prev = m_scr[h]
        m_new = jnp.maximum(m_prev, cm_ref[...])
        al_ref[...] = jnp.exp2(m_prev - m_new)
        m_scr[h] = m_new
        p_ref[...] = jnp.exp2(s_ref[...] - m_new).astype(BF16)

    def accumulate(h, c, buf):
        _, _, p_ref, al_ref = buf
        if stabilize:
            pv = jnp.dot(vt_ref[h, c], p_ref[...], preferred_element_type=F32)
            acc_scr[h] = al_ref[...] * acc_scr[h] + pv
        else:
            pv = jnp.dot(vt_ref[h, c, :DIFF_V_DIM, :], p_ref[...], preferred_element_type=F32)
            acc_scr[h, :DIFF_V_DIM, :] = acc_scr[h, :DIFF_V_DIM, :] + pv

    def finalize(h):
        acc = acc_scr[h]
        total = acc[DIFF_V_DIM:DIFF_V_DIM + 1] if stabilize else m_scr[h]
        o = acc[:DIFF_V_DIM] * (1.0 / total)
        d = o[:, :tq] - sc_ref[0] * o[:, tq:]
        ms = jnp.mean(d * d, axis=0, keepdims=True)
        y = d * lax.rsqrt(ms + SUBLN_EPS) * (g_ref[...] * sc_ref[1])
        o_ref[:, head_cols(h)] = y.T.astype(o_ref.dtype)

    def turn(h, c, parity, nxt, prev):
        if nxt is not None:
            if nxt[0] != h:
                prepare(nxt[0])
            scores(nxt[0], nxt[1], bufs[1 - parity])
        if not stabilize:
            accumulate(h, c, bufs[parity])
            if nxt is None or nxt[0] != h:
                finalize(h)
            return
        softmax(h, bufs[parity])
        if prev is not None:
            accumulate(prev[0], prev[1], bufs[1 - parity])
            if prev[0] != h:
                finalize(prev[0])

    prepare(0)
    scores(0, 0, bufs[0])
    if heads == 1 and n_chunks > 4 and (n_chunks - 4) % 4 == 0:
        turn(0, 0, 0, (0, 1), None)
        turn(0, 1, 1, (0, 2), (0, 0))

        def body(i, carry):
            for j in range(4):
                c = 2 + 4 * i + j
                turn(0, c, j % 2, (0, c + 1), (0, c - 1))
            return carry
        lax.fori_loop(0, (n_chunks - 4) // 4, body, 0)
        turn(0, n_chunks - 2, 0, (0, n_chunks - 1), (0, n_chunks - 3))
        turn(0, n_chunks - 1, 1, None, (0, n_chunks - 2))
    else:
        tasks = [(h, c) for h in range(heads) for c in range(n_chunks)]
        for t, (h, c) in enumerate(tasks):
            turn(h, c, t % 2, tasks[t + 1] if t + 1 < len(tasks) else None, tasks[t - 1] if t else None)
    if stabilize:
        accumulate(heads - 1, n_chunks - 1, bufs[1])
        finalize(heads - 1)


def _diffattn(proj, vt, scalars, subln_col, batch, seq_len, *, stabilize, tq=512, max_unrolled_tasks=32):
    n = proj.shape[0]
    tk = vt.shape[-1]
    n_chunks = seq_len // tk
    q_tiles = seq_len // tq
    assert n_chunks % 2 == 0
    heads = DIFF_HEADS if DIFF_HEADS * n_chunks <= max_unrolled_tasks else 1
    width = heads * DIFF_V_DIM
    return pl.pallas_call(
        functools.partial(_diff_kernel, tk=tk, heads=heads, stabilize=stabilize),
        grid=(batch, DIFF_HEADS // heads, q_tiles),
        in_specs=[
            pl.BlockSpec(memory_space=pltpu.SMEM),
            pl.BlockSpec((tq, width), lambda b, h, i: (b * q_tiles + i, QB_OFF // width + h)),
            pl.BlockSpec((seq_len, width), lambda b, h, i: (b, KB_OFF // width + h)),
            pl.BlockSpec((None, heads, n_chunks, DIFF_VT_ROWS, tk), lambda b, h, i: (b, h, 0, 0, 0)),
            pl.BlockSpec((DIFF_V_DIM, 1), lambda b, h, i: (0, 0)),
        ],
        out_specs=pl.BlockSpec((tq, width), lambda b, h, i: (b * q_tiles + i, h)),
        out_shape=jax.ShapeDtypeStruct((n, DIFF_HEADS * DIFF_V_DIM), BF16),
        scratch_shapes=[
            pltpu.VMEM((heads, DIFF_V_DIM, 2 * tq), BF16),
            pltpu.VMEM((tk, 2 * tq), F32),
            pltpu.VMEM((tk, 2 * tq), F32),
            pltpu.VMEM((1, 2 * tq), F32),
            pltpu.VMEM((1, 2 * tq), F32),
            pltpu.VMEM((tk, 2 * tq), BF16),
            pltpu.VMEM((tk, 2 * tq), BF16),
            pltpu.VMEM((1, 2 * tq), F32),
            pltpu.VMEM((1, 2 * tq), F32),
            pltpu.VMEM((heads, 1, 2 * tq), F32),
            pltpu.VMEM((heads, DIFF_VT_ROWS, 2 * tq), F32),
        ],
        compiler_params=_params("parallel", "parallel", "arbitrary"),
        name="diffattn",
    )(scalars, proj, proj, vt, subln_col)


def _merge_ffn_kernel(x_ref, ya_ref, yb_ref, ga_ref, gb_ref, wa_ref, wb_ref, wo_ref,
                      g_ref, ffn_wi_ref, ffn_wo_ref, o_ref, *, chunk):
    def stages(rows):
        a = jnp.dot(ya_ref[rows, :], wa_ref[...], preferred_element_type=F32)
        b = jnp.dot(yb_ref[rows, :], wb_ref[...], preferred_element_type=F32)
        yield
        merged = ga_ref[rows, :].astype(F32) * a + gb_ref[rows, :].astype(F32) * b
        x = x_ref[rows, :] + jnp.dot(merged.astype(BF16), wo_ref[...], preferred_element_type=F32)
        yield
        o_ref[rows, :] = yield from _ffn_stages(x, g_ref, ffn_wi_ref, ffn_wo_ref, chunk)

    half = x_ref.shape[0] // 2
    _interleave(stages(slice(0, half)), stages(slice(half, 2 * half)))


def _merge_ffn(x, ya, yb, proj, wa, wb, wo, g, ffn_wi, ffn_wo, *, tm=512, chunk=256):
    n = x.shape[0]
    return pl.pallas_call(
        functools.partial(_merge_ffn_kernel, chunk=chunk),
        grid=(n // tm,),
        in_specs=[
            pl.BlockSpec((tm, D_MODEL), lambda i: (i, 0)),
            pl.BlockSpec((tm, NA_WIDTH), lambda i: (i, 0)),
            pl.BlockSpec((tm, NA_WIDTH), lambda i: (i, 0)),
            pl.BlockSpec((tm, D_MODEL), lambda i: (i, GA_OFF // D_MODEL)),
            pl.BlockSpec((tm, D_MODEL), lambda i: (i, GB_OFF // D_MODEL)),
            pl.BlockSpec((NA_WIDTH, D_MODEL), lambda i: (0, 0)),
            pl.BlockSpec((NA_WIDTH, D_MODEL), lambda i: (0, 0)),
            pl.BlockSpec((D_MODEL, D_MODEL), lambda i: (0, 0)),
            pl.BlockSpec((1, D_MODEL), lambda i: (0, 0)),
            pl.BlockSpec((D_MODEL, 2 * D_FF), lambda i: (0, 0)),
            pl.BlockSpec((D_FF, D_MODEL), lambda i: (0, 0)),
        ],
        out_specs=pl.BlockSpec((tm, D_MODEL), lambda i: (i, 0)),
        out_shape=jax.ShapeDtypeStruct((n, D_MODEL), F32),
        compiler_params=_params("parallel"),
        name="merge_ffn",
    )(x, ya, yb, proj, proj, wa, wb, wo, g, ffn_wi, ffn_wo)


def _rope_tables(seq_len):
    half = ROPE_DIM // 2
    inv_freq = jnp.power(ROPE_THETA, -jnp.arange(half, dtype=F32) * 2.0 / ROPE_DIM)
    ang = jnp.arange(seq_len, dtype=F32)[:, None] * inv_freq[None, :]
    cos, sin = jnp.cos(ang), jnp.sin(ang)
    ones = jnp.ones((seq_len, HEAD_DIM - ROPE_DIM), F32)
    zeros = jnp.zeros((seq_len, HEAD_DIM - ROPE_DIM), F32)
    zh = jnp.zeros_like(sin)
    c = jnp.concatenate([cos, cos, ones], axis=1)
    sa = jnp.concatenate([-sin, zh, zeros], axis=1)
    sb = jnp.concatenate([zh, sin, zeros], axis=1)
    reps = V7X_LANES // HEAD_DIM
    return tuple(jnp.tile(t, (1, reps)) for t in (c, sa, sb))


def _layer_consts(l, w):
    scale = HEAD_DIM ** -0.5
    heads = PROJ_BLOCK // HEAD_DIM
    head_gains = jnp.stack([
        jnp.tile(w["qa_norm"][l] * (scale * LOG2_E), heads), jnp.tile(w["ka_norm"][l], heads),
        jnp.tile(w["qb_norm"][l] * (scale * LOG2_E), heads), jnp.tile(w["kb_norm"][l], heads)])
    blk = np.arange(PROJ_BLOCK) // HEAD_DIM
    blockdiag = jnp.asarray((blk[:, None] == blk[None, :]).astype(np.float32) / HEAD_DIM, BF16)
    lam_init = 0.8 - 0.6 * math.exp(-0.3 * l)
    lam = (jnp.exp(jnp.sum(w["lam_q1"][l] * w["lam_k1"][l]))
           - jnp.exp(jnp.sum(w["lam_q2"][l] * w["lam_k2"][l])) + lam_init)
    score_bound = (HEAD_DIM * NORM_SLACK * jnp.max(jnp.abs(w["qb_norm"][l])) * (scale * LOG2_E)
                   * jnp.max(jnp.abs(w["kb_norm"][l])))
    return dict(
        score_bound=score_bound,
        ffn1_g=w["ffn1_norm"][l][None], ffn1_wi=w["ffn1_wi"][l].astype(BF16), ffn1_wo=w["ffn1_wo"][l].astype(BF16),
        mix_g=w["mix_norm"][l][None], w_in=w["w_in"][l].astype(BF16),
        head_gains=head_gains, blockdiag=blockdiag, bias=_natten_bias(w["rpb"][l] * LOG2_E),
        scalars=jnp.stack([lam, jnp.asarray(1.0 - lam_init, F32)]).astype(F32),
        subln=w["subln"][l][:, None],
        w_a=w["w_a_out"][l].astype(BF16), w_b=w["w_b_out"][l].astype(BF16), w_o=w["w_o"][l].astype(BF16),
        ffn2_g=w["ffn2_norm"][l][None], ffn2_wi=w["ffn2_wi"][l].astype(BF16), ffn2_wo=w["ffn2_wo"][l].astype(BF16),
    )


def _encoder_layer(x, c, rope, batch, seq_len):
    x, proj, vt = _ffn_proj(x, c["ffn1_g"], c["ffn1_wi"], c["ffn1_wo"], c["mix_g"], c["w_in"], c["head_gains"],
                            c["blockdiag"], *rope, batch, seq_len, tk=_diff_key_chunk(seq_len))
    ya = _natten(proj, c["bias"], batch, seq_len)
    yb = lax.cond(
        c["score_bound"] <= DIFF_MAX_UNSTABILIZED_SCORE,
        lambda: _diffattn(proj, vt, c["scalars"], c["subln"], batch, seq_len, stabilize=False),
        lambda: _diffattn(proj, vt, c["scalars"], c["subln"], batch, seq_len, stabilize=True))
    return _merge_ffn(x, ya, yb, proj, c["w_a"], c["w_b"], c["w_o"], c["ffn2_g"], c["ffn2_wi"], c["ffn2_wo"])


def _encoder(groups, weights, depth):
    consts = [_layer_consts(l, weights) for l in range(depth)]
    outs = []
    for x in groups:
        batch, seq_len, _ = x.shape
        rope = _rope_tables(seq_len)
        y = x.reshape(batch * seq_len, D_MODEL)
        for c in consts:
            y = _encoder_layer(y, c, rope, batch, seq_len)
        outs.append(y.reshape(x.shape))
    return tuple(outs)


def kernel(x_prompt, x_sample, ffn1_norm, ffn1_wi, ffn1_wo, mix_norm, w_in, qa_norm, ka_norm, rpb, qb_norm, kb_norm, lam_q1, lam_k1, lam_q2, lam_k2, subln, w_a_out, w_b_out, w_o, ffn2_norm, ffn2_wi, ffn2_wo):
    weights = dict(ffn1_norm=ffn1_norm, ffn1_wi=ffn1_wi, ffn1_wo=ffn1_wo, mix_norm=mix_norm, w_in=w_in,
                   qa_norm=qa_norm, ka_norm=ka_norm, rpb=rpb, qb_norm=qb_norm, kb_norm=kb_norm,
                   lam_q1=lam_q1, lam_k1=lam_k1, lam_q2=lam_q2, lam_k2=lam_k2, subln=subln,
                   w_a_out=w_a_out, w_b_out=w_b_out, w_o=w_o,
                   ffn2_norm=ffn2_norm, ffn2_wi=ffn2_wi, ffn2_wo=ffn2_wo)
    return _encoder((x_prompt, x_sample), weights, ffn1_norm.shape[0])
```

```python
import functools
import math

import jax
import jax.numpy as jnp
import numpy as np
from jax import lax
from jax.experimental import pallas as pl
from jax.experimental.pallas import tpu as pltpu

F32 = jnp.float32
BF16 = jnp.bfloat16

D_MODEL = 1024
GRID_W = 64
HEAD_DIM = 64
NA_HEADS = 8
DIFF_HEADS = 4
NA_WIDTH = NA_HEADS * HEAD_DIM
DIFF_V_DIM = 2 * HEAD_DIM
IN_WIDTH = 5120
D_FF = 2816
WIN_H = 8
WIN_W = 16
ROPE_THETA = 500000.0
ROPE_DIM = HEAD_DIM // 4
NORM_EPS = 1e-6
SUBLN_EPS = 1e-5
NEG_INF = -1e30
FFN_RES = 0.5

V7X_VMEM_BYTES = 64 * 1024 * 1024
V7X_LANES = 128
VMEM_LIMIT = V7X_VMEM_BYTES * 7 // 8

PROJ_BLOCK = 512
W_QA, W_KA, W_VA, W_QB, W_KB, W_VB, W_GA, W_GB = 0, 512, 1024, 1536, 2048, 2560, 3072, 4096
GA_OFF, GB_OFF, QA_OFF, KA_OFF, VA_OFF, QB_OFF, KB_OFF = 0, 1024, 2048, 2560, 3072, 3584, 4096
PROJ_OUT = 4608
DIFF_TK_CHOICES = (1024, 512, 256)
DIFF_MIN_CHUNKS = 8
BF16_SUBLANES = 16
DIFF_VT_ROWS = DIFF_V_DIM + BF16_SUBLANES
LOG2_E = math.log2(math.e)
NORM_SLACK = 1.05
DIFF_MAX_UNSTABILIZED_SCORE = 32.0


def _params(*semantics):
    return pltpu.CompilerParams(dimension_semantics=semantics, vmem_limit_bytes=VMEM_LIMIT)


def _rms(x, g, eps):
    return x * lax.rsqrt(jnp.mean(x * x, axis=-1, keepdims=True) + eps) * g


def _interleave(*stage_generators):
    live = list(stage_generators)
    while live:
        for gen in list(live):
            try:
                next(gen)
            except StopIteration:
                live.remove(gen)


def _ffn_stages(x, g_ref, wi_ref, wo_ref, chunk):
    h = _rms(x, g_ref[...], NORM_EPS).astype(BF16)
    n_chunks = D_FF // chunk

    def up(c):
        a = jnp.dot(h, wi_ref[:, c * chunk:(c + 1) * chunk], preferred_element_type=F32)
        b = jnp.dot(h, wi_ref[:, D_FF + c * chunk:D_FF + (c + 1) * chunk], preferred_element_type=F32)
        return a, b

    acc = jnp.zeros(x.shape, F32)
    nxt = up(0)
    yield
    for c in range(n_chunks):
        a, b = nxt
        if c + 1 < n_chunks:
            nxt = up(c + 1)
        gate = (a * jax.nn.sigmoid(a) * b).astype(BF16)
        acc = acc + jnp.dot(gate, wo_ref[c * chunk:(c + 1) * chunk, :], preferred_element_type=F32)
        yield
    return x + FFN_RES * acc


def _ffn_proj_kernel(x_ref, ffn_g_ref, ffn_wi_ref, ffn_wo_ref, g_ref, w_ref, hg_ref, bd_ref,
                     cos_ref, sa_ref, sb_ref, x_out_ref, o_ref, vt_ref, *, chunk):
    tile_rows = x_ref.shape[0]
    n_slabs, slab = vt_ref.shape[1], vt_ref.shape[-1]

    def stages(r0, nrows):
        rows = slice(r0, r0 + nrows)
        x = yield from _ffn_stages(x_ref[rows, :], ffn_g_ref, ffn_wi_ref, ffn_wo_ref, chunk)
        x_out_ref[rows, :] = x
        h = _rms(x, g_ref[...], NORM_EPS).astype(BF16)

        def proj(w_off):
            return jnp.dot(h, w_ref[:, w_off:w_off + PROJ_BLOCK], preferred_element_type=F32)

        def head_norm(p, gain_row):
            ms = jnp.dot((p * p).astype(BF16), bd_ref[...], preferred_element_type=F32)
            return p * lax.rsqrt(ms + NORM_EPS) * hg_ref[gain_row:gain_row + 1, :]

        reps = PROJ_BLOCK // V7X_LANES
        cos = jnp.concatenate([cos_ref[rows, :]] * reps, axis=1)
        sa = jnp.concatenate([sa_ref[rows, :]] * reps, axis=1)
        sb = jnp.concatenate([sb_ref[rows, :]] * reps, axis=1)

        def rope(y):
            half = ROPE_DIM // 2
            up = pltpu.roll(y, PROJ_BLOCK - half, 1)
            dn = pltpu.roll(y, half, 1)
            return y * cos + up * sa + dn * sb

        def put(off, val):
            o_ref[rows, off:off + PROJ_BLOCK] = val.astype(o_ref.dtype)

        def put_diff_values(vb):
            width = min(nrows, slab)
            pad_row = lax.broadcasted_iota(jnp.int32, (BF16_SUBLANES, width), 0)
            ones_rows = jnp.where(pad_row == 0, 1.0, 0.0).astype(vt_ref.dtype)
            for head in range(DIFF_HEADS):
                for start in range(0, nrows, width):
                    c, lane0 = (r0 + start) // slab, (r0 + start) % slab
                    blk = vb[start:start + width, head * DIFF_V_DIM:(head + 1) * DIFF_V_DIM]
                    vt_ref[head, c, :DIFF_V_DIM, lane0:lane0 + width] = blk.T.astype(vt_ref.dtype)
                    vt_ref[head, c, DIFF_V_DIM:, lane0:lane0 + width] = ones_rows

        blocks = [
            (W_QA, lambda p: put(QA_OFF, head_norm(p, 0))),
            (W_GA, lambda p: put(GA_OFF, jax.nn.sigmoid(p))),
            (W_KA, lambda p: put(KA_OFF, head_norm(p, 1))),
            (W_GA + PROJ_BLOCK, lambda p: put(GA_OFF + PROJ_BLOCK, jax.nn.sigmoid(p))),
            (W_QB, lambda p: put(QB_OFF, rope(head_norm(p, 2)))),
            (W_GB, lambda p: put(GB_OFF, jax.nn.sigmoid(p))),
            (W_KB, lambda p: put(KB_OFF, rope(head_norm(p, 3)))),
            (W_GB + PROJ_BLOCK, lambda p: put(GB_OFF + PROJ_BLOCK, jax.nn.sigmoid(p))),
            (W_VB, put_diff_values),
            (W_VA, lambda p: put(VA_OFF, p)),
        ]
        nxt = proj(blocks[0][0])
        yield
        for n, (_, epilogue) in enumerate(blocks):
            p = nxt
            if n + 1 < len(blocks):
                nxt = proj(blocks[n + 1][0])
            epilogue(p)
            yield

    assert n_slabs * slab == tile_rows
    half = tile_rows // 2
    _interleave(stages(0, half), stages(half, half))


def _diff_key_chunk(seq_len):
    fits = [t for t in DIFF_TK_CHOICES if seq_len % (2 * t) == 0]
    deep = [t for t in fits if seq_len // t >= DIFF_MIN_CHUNKS]
    return (deep or fits[-1:])[0]


def _ffn_proj(x, ffn_g, ffn_wi, ffn_wo, g, w, head_gains, blockdiag, cos, sa, sb, batch, seq_len, *,
              tk, tm=512, chunk=256):
    n = x.shape[0]
    tiles_per_seq = seq_len // tm
    if tm >= tk:
        vt_spec = pl.BlockSpec((None, DIFF_HEADS, tm // tk, DIFF_VT_ROWS, tk),
                               lambda i: (i // tiles_per_seq, 0, i % tiles_per_seq, 0, 0))
    else:
        sub = tk // tm
        vt_spec = pl.BlockSpec((None, DIFF_HEADS, 1, DIFF_VT_ROWS, tm),
                               lambda i: (i // tiles_per_seq, 0, (i % tiles_per_seq) // sub, 0,
                                          (i % tiles_per_seq) % sub))
    const = lambda i: (0, 0)
    return pl.pallas_call(
        functools.partial(_ffn_proj_kernel, chunk=chunk),
        grid=(n // tm,),
        in_specs=[
            pl.BlockSpec((tm, D_MODEL), lambda i: (i, 0)),
            pl.BlockSpec((1, D_MODEL), const),
            pl.BlockSpec((D_MODEL, 2 * D_FF), const),
            pl.BlockSpec((D_FF, D_MODEL), const),
            pl.BlockSpec((1, D_MODEL), const),
            pl.BlockSpec((D_MODEL, IN_WIDTH), const),
            pl.BlockSpec((4, PROJ_BLOCK), const),
            pl.BlockSpec((PROJ_BLOCK, PROJ_BLOCK), const),
            pl.BlockSpec((tm, V7X_LANES), lambda i: (i % tiles_per_seq, 0)),
            pl.BlockSpec((tm, V7X_LANES), lambda i: (i % tiles_per_seq, 0)),
            pl.BlockSpec((tm, V7X_LANES), lambda i: (i % tiles_per_seq, 0)),
        ],
        out_specs=[
            pl.BlockSpec((tm, D_MODEL), lambda i: (i, 0)),
            pl.BlockSpec((tm, PROJ_OUT), lambda i: (i, 0)),
            vt_spec,
        ],
        out_shape=[
            jax.ShapeDtypeStruct((n, D_MODEL), F32),
            jax.ShapeDtypeStruct((n, PROJ_OUT), BF16),
            jax.ShapeDtypeStruct((batch, DIFF_HEADS, seq_len // tk, DIFF_VT_ROWS, tk), BF16),
        ],
        compiler_params=_params("parallel"),
        name="ffn_proj",
    )(x, ffn_g, ffn_wi, ffn_wo, g, w, head_gains, blockdiag, cos, sa, sb)


def _natten_kernel(q_ref, k_ref, v_ref, bias_ref, o_ref, *, rows, rows_per_step, lookahead):
    n_keys = WIN_H * GRID_W
    r0 = pl.program_id(1) * rows_per_step
    block_start = jnp.clip(r0 - WIN_H // 2, 0, rows - (rows_per_step + WIN_H - 1))
    lane = lax.broadcasted_iota(jnp.int32, (GRID_W, V7X_LANES), 1)
    lo = lane < HEAD_DIM
    tasks = [(i, pair) for i in range(rows_per_step) for pair in range(NA_HEADS // 2)]

    def key_rows(i):
        start = jnp.clip(r0 + i - WIN_H // 2, 0, rows - WIN_H)
        variant = start - (r0 + i) + WIN_H - 1
        return pl.ds(pl.multiple_of((start - block_start) * GRID_W, GRID_W), n_keys), variant

    def scores(i, pair):
        keys, variant = key_rows(i)
        cols = slice(pair * V7X_LANES, (pair + 1) * V7X_LANES)
        qp = q_ref[i * GRID_W:(i + 1) * GRID_W, cols]
        zero = jnp.zeros_like(qp)
        lhs = jnp.concatenate([jnp.where(lo, qp, zero), jnp.where(lo, zero, qp)], axis=0)
        s = lax.dot_general(lhs, k_ref[keys, cols], (((1,), (1,)), ((), ())), preferred_element_type=F32)
        return s + bias_ref[variant, pair * 2 * GRID_W:(pair + 1) * 2 * GRID_W, :]

    pending = [scores(*task) for task in tasks[:lookahead]]
    for n, (i, pair) in enumerate(tasks):
        s = pending.pop(0)
        if n + lookahead < len(tasks):
            pending.append(scores(*tasks[n + lookahead]))
        keys, _ = key_rows(i)
        cols = slice(pair * V7X_LANES, (pair + 1) * V7X_LANES)
        m = jnp.max(s, axis=-1, keepdims=True)
        e = jnp.exp2(s - m)
        l = jnp.sum(e, axis=-1, keepdims=True)
        o = jnp.dot(e.astype(BF16), v_ref[keys, cols], preferred_element_type=F32)
        o = o * (1.0 / l)
        o_ref[i * GRID_W:(i + 1) * GRID_W, cols] = jnp.where(lo, o[:GRID_W], o[GRID_W:]).astype(o_ref.dtype)


def _natten(proj, bias, batch, seq_len, *, max_rows_per_step=16, lookahead=3):
    rows = seq_len // GRID_W
    n = proj.shape[0]
    fits = [r for r in (16, 8, 4, 2, 1) if r <= max_rows_per_step and rows % r == 0 and rows >= r + WIN_H - 1]
    assert fits, "sequence shorter than one neighbourhood window"
    rows_per_step = fits[0]
    block_rows = rows_per_step + WIN_H - 1
    steps = rows // rows_per_step

    def block_start(j):
        return jnp.clip(j * rows_per_step - WIN_H // 2, 0, rows - block_rows)

    return pl.pallas_call(
        functools.partial(_natten_kernel, rows=rows, rows_per_step=rows_per_step, lookahead=lookahead),
        grid=(batch, steps),
        in_specs=[
            pl.BlockSpec((rows_per_step * GRID_W, NA_WIDTH), lambda b, j: (b * steps + j, QA_OFF // NA_WIDTH)),
            pl.BlockSpec((pl.Element(block_rows * GRID_W), pl.Element(NA_WIDTH)),
                         lambda b, j: ((b * rows + block_start(j)) * GRID_W, KA_OFF)),
            pl.BlockSpec((pl.Element(block_rows * GRID_W), pl.Element(NA_WIDTH)),
                         lambda b, j: ((b * rows + block_start(j)) * GRID_W, VA_OFF)),
            pl.BlockSpec((WIN_H, NA_HEADS * GRID_W, WIN_H * GRID_W), lambda b, j: (0, 0, 0)),
        ],
        out_specs=pl.BlockSpec((rows_per_step * GRID_W, NA_WIDTH), lambda b, j: (b * steps + j, 0)),
        out_shape=jax.ShapeDtypeStruct((n, NA_WIDTH), BF16),
        compiler_params=_params("parallel", "arbitrary"),
        name="natten",
    )(proj, proj, proj, bias)


def _natten_bias(rpb):
    qc = np.arange(GRID_W)[:, None]
    kc = np.arange(GRID_W)[None, :]
    wstart = np.clip(qc - WIN_W // 2, 0, GRID_W - WIN_W)
    in_win = (kc >= wstart) & (kc < wstart + WIN_W)
    dc = np.clip(kc - qc + WIN_W - 1, 0, 2 * WIN_W - 2)
    dr = np.arange(WIN_H)[:, None] + np.arange(WIN_H)[None, :]
    pick_dc = (dc[..., None] == np.arange(2 * WIN_W - 1)).astype(np.float32)
    pick_dr = (dr[..., None] == np.arange(2 * WIN_H - 1)).astype(np.float32)
    b = jnp.einsum("tsr,hrd,qkd->thqsk", pick_dr, rpb.astype(F32), pick_dc,
                   precision=lax.Precision.HIGHEST)
    b = jnp.where(in_win[None, None, :, None, :], b, NEG_INF)
    return b.reshape(WIN_H, NA_HEADS * GRID_W, WIN_H * GRID_W)


def _diff_kernel(sc_ref, q_ref, k_ref, vt_ref, g_ref, o_ref,
                 qst_scr, s_a, s_b, cm_a, cm_b, p_a, p_b, al_a, al_b, m_scr, acc_scr, *, tk, heads, stabilize):
    tq = q_ref.shape[0]
    n_chunks = k_ref.shape[0] // tk
    bufs = ((s_a, cm_a, p_a, al_a), (s_b, cm_b, p_b, al_b))

    def head_cols(h):
        return slice(h * DIFF_V_DIM, (h + 1) * DIFF_V_DIM)

    def prepare(h):
        qt = q_ref[:, head_cols(h)].astype(F32).T
        row = lax.broadcasted_iota(jnp.int32, qt.shape, 0)
        zero = jnp.zeros_like(qt)
        qst_scr[h] = jnp.concatenate(
            [jnp.where(row < HEAD_DIM, qt, zero), jnp.where(row < HEAD_DIM, zero, qt)], axis=1).astype(BF16)
        m_scr[h] = jnp.full(m_scr.shape[1:], NEG_INF if stabilize else 0.0, F32)
        acc_scr[h] = jnp.zeros(acc_scr.shape[1:], F32)

    def scores(h, c, buf):
        s_ref, cm_ref, p_ref, _ = buf
        start = pl.multiple_of(c * tk, tk)
        s = jnp.dot(k_ref[pl.ds(start, tk), head_cols(h)], qst_scr[h], preferred_element_type=F32)
        if stabilize:
            s_ref[...] = s
            cm_ref[...] = jnp.max(s, axis=0, keepdims=True)
        else:
            p = jnp.exp2(s)
            p_ref[...] = p.astype(BF16)
            m_scr[h] = m_scr[h] + jnp.sum(p, axis=0, keepdims=True)

    def softmax(h, buf):
        s_ref, cm_ref, p_ref, al_ref = buf
        m_prev = m_scr[h]
        m_new = jnp.maximum(m_prev, cm_ref[...])
        al_ref[...] = jnp.exp2(m_prev - m_new)
        m_scr[h] = m_new
        p_ref[...] = jnp.exp2(s_ref[...] - m_new).astype(BF16)

    def accumulate(h, c, buf):
        _, _, p_ref, al_ref = buf
        if stabilize:
            pv = jnp.dot(vt_ref[h, c], p_ref[...], preferred_element_type=F32)
            acc_scr[h] = al_ref[...] * acc_scr[h] + pv
        else:
            pv = jnp.dot(vt_ref[h, c, :DIFF_V_DIM, :], p_ref[...], preferred_element_type=F32)
            acc_scr[h, :DIFF_V_DIM, :] = acc_scr[h, :DIFF_V_DIM, :] + pv

    def finalize(h):
        acc = acc_scr[h]
        total = acc[DIFF_V_DIM:DIFF_V_DIM + 1] if stabilize else m_scr[h]
        o = acc[:DIFF_V_DIM] * (1.0 / total)
        d = o[:, :tq] - sc_ref[0] * o[:, tq:]
        ms = jnp.mean(d * d, axis=0, keepdims=True)
        y = d * lax.rsqrt(ms + SUBLN_EPS) * (g_ref[...] * sc_ref[1])
        o_ref[:, head_cols(h)] = y.T.astype(o_ref.dtype)

    def turn(h, c, parity, nxt, prev):
        if nxt is not None:
            if nxt[0] != h:
                prepare(nxt[0])
            scores(nxt[0], nxt[1], bufs[1 - parity])
        if not stabilize:
            accumulate(h, c, bufs[parity])
            if nxt is None or nxt[0] != h:
                finalize(h)
            return
        softmax(h, bufs[parity])
        if prev is not None:
            accumulate(prev[0], prev[1], bufs[1 - parity])
            if prev[0] != h:
                finalize(prev[0])

    prepare(0)
    scores(0, 0, bufs[0])
    if heads == 1 and n_chunks > 4 and (n_chunks - 4) % 4 == 0:
        turn(0, 0, 0, (0, 1), None)
        turn(0, 1, 1, (0, 2), (0, 0))

        def body(i, carry):
            for j in range(4):
                c = 2 + 4 * i + j
                turn(0, c, j % 2, (0, c + 1), (0, c - 1))
            return carry
        lax.fori_loop(0, (n_chunks - 4) // 4, body, 0)
        turn(0, n_chunks - 2, 0, (0, n_chunks - 1), (0, n_chunks - 3))
        turn(0, n_chunks - 1, 1, None, (0, n_chunks - 2))
    else:
        tasks = [(h, c) for h in range(heads) for c in range(n_chunks)]
        for t, (h, c) in enumerate(tasks):
            turn(h, c, t % 2, tasks[t + 1] if t + 1 < len(tasks) else None, tasks[t - 1] if t else None)
    if stabilize:
        accumulate(heads - 1, n_chunks - 1, bufs[1])
        finalize(heads - 1)


def _diffattn(proj, vt, scalars, subln_col, batch, seq_len, *, stabilize, tq=512, max_unrolled_tasks=32):
    n = proj.shape[0]
    tk = vt.shape[-1]
    n_chunks = seq_len // tk
    q_tiles = seq_len // tq
    assert n_chunks % 2 == 0
    heads = DIFF_HEADS if DIFF_HEADS * n_chunks <= max_unrolled_tasks else 1
    width = heads * DIFF_V_DIM
    return pl.pallas_call(
        functools.partial(_diff_kernel, tk=tk, heads=heads, stabilize=stabilize),
        grid=(batch, DIFF_HEADS // heads, q_tiles),
        in_specs=[
            pl.BlockSpec(memory_space=pltpu.SMEM),
            pl.BlockSpec((tq, width), lambda b, h, i: (b * q_tiles + i, QB_OFF // width + h)),
            pl.BlockSpec((seq_len, width), lambda b, h, i: (b, KB_OFF // width + h)),
            pl.BlockSpec((None, heads, n_chunks, DIFF_VT_ROWS, tk), lambda b, h, i: (b, h, 0, 0, 0)),
            pl.BlockSpec((DIFF_V_DIM, 1), lambda b, h, i: (0, 0)),
        ],
        out_specs=pl.BlockSpec((tq, width), lambda b, h, i: (b * q_tiles + i, h)),
        out_shape=jax.ShapeDtypeStruct((n, DIFF_HEADS * DIFF_V_DIM), BF16),
        scratch_shapes=[
            pltpu.VMEM((heads, DIFF_V_DIM, 2 * tq), BF16),
            pltpu.VMEM((tk, 2 * tq), F32),
            pltpu.VMEM((tk, 2 * tq), F32),
            pltpu.VMEM((1, 2 * tq), F32),
            pltpu.VMEM((1, 2 * tq), F32),
            pltpu.VMEM((tk, 2 * tq), BF16),
            pltpu.VMEM((tk, 2 * tq), BF16),
            pltpu.VMEM((1, 2 * tq), F32),
            pltpu.VMEM((1, 2 * tq), F32),
            pltpu.VMEM((heads, 1, 2 * tq), F32),
            pltpu.VMEM((heads, DIFF_VT_ROWS, 2 * tq), F32),
        ],
        compiler_params=_params("parallel", "parallel", "arbitrary"),
        name="diffattn",
    )(scalars, proj, proj, vt, subln_col)


def _merge_ffn_kernel(x_ref, ya_ref, yb_ref, ga_ref, gb_ref, wa_ref, wb_ref, wo_ref,
                      g_ref, ffn_wi_ref, ffn_wo_ref, o_ref, *, chunk):
    def stages(rows):
        a = jnp.dot(ya_ref[rows, :], wa_ref[...], preferred_element_type=F32)
        b = jnp.dot(yb_ref[rows, :], wb_ref[...], preferred_element_type=F32)
        yield
        merged = ga_ref[rows, :].astype(F32) * a + gb_ref[rows, :].astype(F32) * b
        x = x_ref[rows, :] + jnp.dot(merged.astype(BF16), wo_ref[...], preferred_element_type=F32)
        yield
        o_ref[rows, :] = yield from _ffn_stages(x, g_ref, ffn_wi_ref, ffn_wo_ref, chunk)

    half = x_ref.shape[0] // 2
    _interleave(stages(slice(0, half)), stages(slice(half, 2 * half)))


def _merge_ffn(x, ya, yb, proj, wa, wb, wo, g, ffn_wi, ffn_wo, *, tm=512, chunk=256):
    n = x.shape[0]
    return pl.pallas_call(
        functools.partial(_merge_ffn_kernel, chunk=chunk),
        grid=(n // tm,),
        in_specs=[
            pl.BlockSpec((tm, D_MODEL), lambda i: (i, 0)),
            pl.BlockSpec((tm, NA_WIDTH), lambda i: (i, 0)),
            pl.BlockSpec((tm, NA_WIDTH), lambda i: (i, 0)),
            pl.BlockSpec((tm, D_MODEL), lambda i: (i, GA_OFF // D_MODEL)),
            pl.BlockSpec((tm, D_MODEL), lambda i: (i, GB_OFF // D_MODEL)),
            pl.BlockSpec((NA_WIDTH, D_MODEL), lambda i: (0, 0)),
            pl.BlockSpec((NA_WIDTH, D_MODEL), lambda i: (0, 0)),
            pl.BlockSpec((D_MODEL, D_MODEL), lambda i: (0, 0)),
            pl.BlockSpec((1, D_MODEL), lambda i: (0, 0)),
            pl.BlockSpec((D_MODEL, 2 * D_FF), lambda i: (0, 0)),
            pl.BlockSpec((D_FF, D_MODEL), lambda i: (0, 0)),
        ],
        out_specs=pl.BlockSpec((tm, D_MODEL), lambda i: (i, 0)),
        out_shape=jax.ShapeDtypeStruct((n, D_MODEL), F32),
        compiler_params=_params("parallel"),
        name="merge_ffn",
    )(x, ya, yb, proj, proj, wa, wb, wo, g, ffn_wi, ffn_wo)


def _rope_tables(seq_len):
    half = ROPE_DIM // 2
    inv_freq = jnp.power(ROPE_THETA, -jnp.arange(half, dtype=F32) * 2.0 / ROPE_DIM)
    ang = jnp.arange(seq_len, dtype=F32)[:, None] * inv_freq[None, :]
    cos, sin = jnp.cos(ang), jnp.sin(ang)
    ones = jnp.ones((seq_len, HEAD_DIM - ROPE_DIM), F32)
    zeros = jnp.zeros((seq_len, HEAD_DIM - ROPE_DIM), F32)
    zh = jnp.zeros_like(sin)
    c = jnp.concatenate([cos, cos, ones], axis=1)
    sa = jnp.concatenate([-sin, zh, zeros], axis=1)
    sb = jnp.concatenate([zh, sin, zeros], axis=1)
    reps = V7X_LANES // HEAD_DIM
    return tuple(jnp.tile(t, (1, reps)) for t in (c, sa, sb))


def _layer_consts(l, w):
    scale = HEAD_DIM ** -0.5
    heads = PROJ_BLOCK // HEAD_DIM
    head_gains = jnp.stack([
        jnp.tile(w["qa_norm"][l] * (scale * LOG2_E), heads), jnp.tile(w["ka_norm"][l], heads),
        jnp.tile(w["qb_norm"][l] * (scale * LOG2_E), heads), jnp.tile(w["kb_norm"][l], heads)])
    blk = np.arange(PROJ_BLOCK) // HEAD_DIM
    blockdiag = jnp.asarray((blk[:, None] == blk[None, :]).astype(np.float32) / HEAD_DIM, BF16)
    lam_init = 0.8 - 0.6 * math.exp(-0.3 * l)
    lam = (jnp.exp(jnp.sum(w["lam_q1"][l] * w["lam_k1"][l]))
           - jnp.exp(jnp.sum(w["lam_q2"][l] * w["lam_k2"][l])) + lam_init)
    score_bound = (HEAD_DIM * NORM_SLACK * jnp.max(jnp.abs(w["qb_norm"][l])) * (scale * LOG2_E)
                   * jnp.max(jnp.abs(w["kb_norm"][l])))
    return dict(
        score_bound=score_bound,
        ffn1_g=w["ffn1_norm"][l][None], ffn1_wi=w["ffn1_wi"][l].astype(BF16), ffn1_wo=w["ffn1_wo"][l].astype(BF16),
        mix_g=w["mix_norm"][l][None], w_in=w["w_in"][l].astype(BF16),
        head_gains=head_gains, blockdiag=blockdiag, bias=_natten_bias(w["rpb"][l] * LOG2_E),
        scalars=jnp.stack([lam, jnp.asarray(1.0 - lam_init, F32)]).astype(F32),
        subln=w["subln"][l][:, None],
        w_a=w["w_a_out"][l].astype(BF16), w_b=w["w_b_out"][l].astype(BF16), w_o=w["w_o"][l].astype(BF16),
        ffn2_g=w["ffn2_norm"][l][None], ffn2_wi=w["ffn2_wi"][l].astype(BF16), ffn2_wo=w["ffn2_wo"][l].astype(BF16),
    )


def _encoder_layer(x, c, rope, batch, seq_len):
    x, proj, vt = _ffn_proj(x, c["ffn1_g"], c["ffn1_wi"], c["ffn1_wo"], c["mix_g"], c["w_in"], c["head_gains"],
                            c["blockdiag"], *rope, batch, seq_len, tk=_diff_key_chunk(seq_len))
    ya = _natten(proj, c["bias"], batch, seq_len)
    yb = lax.cond(
        c["score_bound"] <= DIFF_MAX_UNSTABILIZED_SCORE,
        lambda: _diffattn(proj, vt, c["scalars"], c["subln"], batch, seq_len, stabilize=False),
        lambda: _diffattn(proj, vt, c["scalars"], c["subln"], batch, seq_len, stabilize=True))
    return _merge_ffn(x, ya, yb, proj, c["w_a"], c["w_b"], c["w_o"], c["ffn2_g"], c["ffn2_wi"], c["ffn2_wo"])


def _encoder(groups, weights, depth):
    consts = [_layer_consts(l, weights) for l in range(depth)]
    outs = []
    for x in groups:
        batch, seq_len, _ = x.shape
        rope = _rope_tables(seq_len)
        y = x.reshape(batch * seq_len, D_MODEL)
        for c in consts:
            y = _encoder_layer(y, c, rope, batch, seq_len)
        outs.append(y.reshape(x.shape))
    return tuple(outs)


def kernel(x_prompt, x_sample, ffn1_norm, ffn1_wi, ffn1_wo, mix_norm, w_in, qa_norm, ka_norm, rpb, qb_norm, kb_norm, lam_q1, lam_k1, lam_q2, lam_k2, subln, w_a_out, w_b_out, w_o, ffn2_norm, ffn2_wi, ffn2_wo):
    weights = dict(ffn1_norm=ffn1_norm, ffn1_wi=ffn1_wi, ffn1_wo=ffn1_wo, mix_norm=mix_norm, w_in=w_in,
                   qa_norm=qa_norm, ka_norm=ka_norm, rpb=rpb, qb_norm=qb_norm, kb_norm=kb_norm,
                   lam_q1=lam_q1, lam_k1=lam_k1, lam_q2=lam_q2, lam_k2=lam_k2, subln=subln,
                   w_a_out=w_a_out, w_b_out=w_b_out, w_o=w_o,
                   ffn2_norm=ffn2_norm, ffn2_wi=ffn2_wi, ffn2_wo=ffn2_wo)
    return _encoder((x_prompt, x_sample), weights, ffn1_norm.shape[0])
```

```python
import functools
import math

import jax
import jax.numpy as jnp
import numpy as np
from jax import lax
from jax.experimental import pallas as pl
from jax.experimental.pallas import tpu as pltpu

F32 = jnp.float32
BF16 = jnp.bfloat16

D_MODEL = 1024
GRID_W = 64
HEAD_DIM = 64
NA_HEADS = 8
DIFF_HEADS = 4
NA_WIDTH = NA_HEADS * HEAD_DIM
DIFF_V_DIM = 2 * HEAD_DIM
IN_WIDTH = 5120
D_FF = 2816
WIN_H = 8
WIN_W = 16
ROPE_THETA = 500000.0
ROPE_DIM = HEAD_DIM // 4
NORM_EPS = 1e-6
SUBLN_EPS = 1e-5
NEG_INF = -1e30
FFN_RES = 0.5

V7X_VMEM_BYTES = 64 * 1024 * 1024
V7X_LANES = 128
VMEM_LIMIT = V7X_VMEM_BYTES * 7 // 8

PROJ_BLOCK = 512
W_QA, W_KA, W_VA, W_QB, W_KB, W_VB, W_GA, W_GB = 0, 512, 1024, 1536, 2048, 2560, 3072, 4096
GA_OFF, GB_OFF, QA_OFF, KA_OFF, VA_OFF, QB_OFF, KB_OFF = 0, 1024, 2048, 2560, 3072, 3584, 4096
PROJ_OUT = 4608
DIFF_TK_CHOICES = (1024, 512, 256)
DIFF_MIN_CHUNKS = 8
BF16_SUBLANES = 16
DIFF_VT_ROWS = DIFF_V_DIM + BF16_SUBLANES
LOG2_E = math.log2(math.e)
NORM_SLACK = 1.05
DIFF_MAX_UNSTABILIZED_SCORE = 32.0


def _params(*semantics):
    return pltpu.CompilerParams(dimension_semantics=semantics, vmem_limit_bytes=VMEM_LIMIT)


def _rms(x, g, eps):
    return x * lax.rsqrt(jnp.mean(x * x, axis=-1, keepdims=True) + eps) * g


def _interleave(*stage_generators):
    live = list(stage_generators)
    while live:
        for gen in list(live):
            try:
                next(gen)
            except StopIteration:
                live.remove(gen)


def _ffn_stages(x, g_ref, wi_ref, wo_ref, chunk):
    h = _rms(x, g_ref[...], NORM_EPS).astype(BF16)
    n_chunks = D_FF // chunk

    def up(c):
        a = jnp.dot(h, wi_ref[:, c * chunk:(c + 1) * chunk], preferred_element_type=F32)
        b = jnp.dot(h, wi_ref[:, D_FF + c * chunk:D_FF + (c + 1) * chunk], preferred_element_type=F32)
        return a, b

    acc = jnp.zeros(x.shape, F32)
    nxt = up(0)
    yield
    for c in range(n_chunks):
        a, b = nxt
        if c + 1 < n_chunks:
            nxt = up(c + 1)
        gate = (a * jax.nn.sigmoid(a) * b).astype(BF16)
        acc = acc + jnp.dot(gate, wo_ref[c * chunk:(c + 1) * chunk, :], preferred_element_type=F32)
        yield
    return x + FFN_RES * acc


def _ffn_proj_kernel(x_ref, ffn_g_ref, ffn_wi_ref, ffn_wo_ref, g_ref, w_ref, hg_ref, bd_ref,
                     cos_ref, sa_ref, sb_ref, x_out_ref, o_ref, vt_ref, *, chunk):
    tile_rows = x_ref.shape[0]
    n_slabs, slab = vt_ref.shape[1], vt_ref.shape[-1]

    def stages(r0, nrows):
        rows = slice(r0, r0 + nrows)
        x = yield from _ffn_stages(x_ref[rows, :], ffn_g_ref, ffn_wi_ref, ffn_wo_ref, chunk)
        x_out_ref[rows, :] = x
        h = _rms(x, g_ref[...], NORM_EPS).astype(BF16)

        def proj(w_off):
            return jnp.dot(h, w_ref[:, w_off:w_off + PROJ_BLOCK], preferred_element_type=F32)

        def head_norm(p, gain_row):
            ms = jnp.dot((p * p).astype(BF16), bd_ref[...], preferred_element_type=F32)
            return p * lax.rsqrt(ms + NORM_EPS) * hg_ref[gain_row:gain_row + 1, :]

        reps = PROJ_BLOCK // V7X_LANES
        cos = jnp.concatenate([cos_ref[rows, :]] * reps, axis=1)
        sa = jnp.concatenate([sa_ref[rows, :]] * reps, axis=1)
        sb = jnp.concatenate([sb_ref[rows, :]] * reps, axis=1)

        def rope(y):
            half = ROPE_DIM // 2
            up = pltpu.roll(y, PROJ_BLOCK - half, 1)
            dn = pltpu.roll(y, half, 1)
            return y * cos + up * sa + dn * sb

        def put(off, val):
            o_ref[rows, off:off + PROJ_BLOCK] = val.astype(o_ref.dtype)

        def put_diff_values(vb):
            width = min(nrows, slab)
            pad_row = lax.broadcasted_iota(jnp.int32, (BF16_SUBLANES, width), 0)
            ones_rows = jnp.where(pad_row == 0, 1.0, 0.0).astype(vt_ref.dtype)
            for head in range(DIFF_HEADS):
                for start in range(0, nrows, width):
                    c, lane0 = (r0 + start) // slab, (r0 + start) % slab
                    blk = vb[start:start + width, head * DIFF_V_DIM:(head + 1) * DIFF_V_DIM]
                    vt_ref[head, c, :DIFF_V_DIM, lane0:lane0 + width] = blk.T.astype(vt_ref.dtype)
                    vt_ref[head, c, DIFF_V_DIM:, lane0:lane0 + width] = ones_rows

        blocks = [
            (W_QA, lambda p: put(QA_OFF, head_norm(p, 0))),
            (W_GA, lambda p: put(GA_OFF, jax.nn.sigmoid(p))),
            (W_KA, lambda p: put(KA_OFF, head_norm(p, 1))),
            (W_GA + PROJ_BLOCK, lambda p: put(GA_OFF + PROJ_BLOCK, jax.nn.sigmoid(p))),
            (W_QB, lambda p: put(QB_OFF, rope(head_norm(p, 2)))),
            (W_GB, lambda p: put(GB_OFF, jax.nn.sigmoid(p))),
            (W_KB, lambda p: put(KB_OFF, rope(head_norm(p, 3)))),
            (W_GB + PROJ_BLOCK, lambda p: put(GB_OFF + PROJ_BLOCK, jax.nn.sigmoid(p))),
            (W_VB, put_diff_values),
            (W_VA, lambda p: put(VA_OFF, p)),
        ]
        nxt = proj(blocks[0][0])
        yield
        for n, (_, epilogue) in enumerate(blocks):
            p = nxt
            if n + 1 < len(blocks):
                nxt = proj(blocks[n + 1][0])
            epilogue(p)
            yield

    assert n_slabs * slab == tile_rows
    half = tile_rows // 2
    _interleave(stages(0, half), stages(half, half))


def _diff_key_chunk(seq_len):
    fits = [t for t in DIFF_TK_CHOICES if seq_len % (2 * t) == 0]
    deep = [t for t in fits if seq_len // t >= DIFF_MIN_CHUNKS]
    return (deep or fits[-1:])[0]


def _ffn_proj(x, ffn_g, ffn_wi, ffn_wo, g, w, head_gains, blockdiag, cos, sa, sb, batch, seq_len, *,
              layer, tk, tm=512, chunk=256):
    n = x.shape[0]
    tiles_per_seq = seq_len // tm
    if tm >= tk:
        vt_spec = pl.BlockSpec((None, DIFF_HEADS, tm // tk, DIFF_VT_ROWS, tk),
                               lambda i: (i // tiles_per_seq, 0, i % tiles_per_seq, 0, 0))
    else:
        sub = tk // tm
        vt_spec = pl.BlockSpec((None, DIFF_HEADS, 1, DIFF_VT_ROWS, tm),
                               lambda i: (i // tiles_per_seq, 0, (i % tiles_per_seq) // sub, 0,
                                          (i % tiles_per_seq) % sub))
    const = lambda i: (0, 0)
    of_layer = lambda i: (layer, 0, 0)
    return pl.pallas_call(
        functools.partial(_ffn_proj_kernel, chunk=chunk),
        grid=(n // tm,),
        in_specs=[
            pl.BlockSpec((tm, D_MODEL), lambda i: (i, 0)),
            pl.BlockSpec((1, D_MODEL), const),
            pl.BlockSpec((None, D_MODEL, 2 * D_FF), of_layer),
            pl.BlockSpec((None, D_FF, D_MODEL), of_layer),
            pl.BlockSpec((1, D_MODEL), const),
            pl.BlockSpec((None, D_MODEL, IN_WIDTH), of_layer),
            pl.BlockSpec((4, PROJ_BLOCK), const),
            pl.BlockSpec((PROJ_BLOCK, PROJ_BLOCK), const),
            pl.BlockSpec((tm, V7X_LANES), lambda i: (i % tiles_per_seq, 0)),
            pl.BlockSpec((tm, V7X_LANES), lambda i: (i % tiles_per_seq, 0)),
            pl.BlockSpec((tm, V7X_LANES), lambda i: (i % tiles_per_seq, 0)),
        ],
        out_specs=[
            pl.BlockSpec((tm, D_MODEL), lambda i: (i, 0)),
            pl.BlockSpec((tm, PROJ_OUT), lambda i: (i, 0)),
            vt_spec,
        ],
        out_shape=[
            jax.ShapeDtypeStruct((n, D_MODEL), F32),
            jax.ShapeDtypeStruct((n, PROJ_OUT), BF16),
            jax.ShapeDtypeStruct((batch, DIFF_HEADS, seq_len // tk, DIFF_VT_ROWS, tk), BF16),
        ],
        compiler_params=_params("parallel"),
        name="ffn_proj",
    )(x, ffn_g, ffn_wi, ffn_wo, g, w, head_gains, blockdiag, cos, sa, sb)


def _natten_kernel(q_ref, k_ref, v_ref, bias_ref, o_ref, *, rows, rows_per_step, lookahead):
    n_keys = WIN_H * GRID_W
    r0 = pl.program_id(1) * rows_per_step
    block_start = jnp.clip(r0 - WIN_H // 2, 0, rows - (rows_per_step + WIN_H - 1))
    lane = lax.broadcasted_iota(jnp.int32, (GRID_W, V7X_LANES), 1)
    lo = lane < HEAD_DIM
    tasks = [(i, pair) for i in range(rows_per_step) for pair in range(NA_HEADS // 2)]

    def key_rows(i):
        start = jnp.clip(r0 + i - WIN_H // 2, 0, rows - WIN_H)
        variant = start - (r0 + i) + WIN_H - 1
        return pl.ds(pl.multiple_of((start - block_start) * GRID_W, GRID_W), n_keys), variant

    def scores(i, pair):
        keys, variant = key_rows(i)
        cols = slice(pair * V7X_LANES, (pair + 1) * V7X_LANES)
        qp = q_ref[i * GRID_W:(i + 1) * GRID_W, cols]
        zero = jnp.zeros_like(qp)
        lhs = jnp.concatenate([jnp.where(lo, qp, zero), jnp.where(lo, zero, qp)], axis=0)
        s = lax.dot_general(lhs, k_ref[keys, cols], (((1,), (1,)), ((), ())), preferred_element_type=F32)
        return s + bias_ref[variant, pair * 2 * GRID_W:(pair + 1) * 2 * GRID_W, :]

    pending = [scores(*task) for task in tasks[:lookahead]]
    for n, (i, pair) in enumerate(tasks):
        s = pending.pop(0)
        if n + lookahead < len(tasks):
            pending.append(scores(*tasks[n + lookahead]))
        keys, _ = key_rows(i)
        cols = slice(pair * V7X_LANES, (pair + 1) * V7X_LANES)
        m = jnp.max(s, axis=-1, keepdims=True)
        e = jnp.exp2(s - m)
        l = jnp.sum(e, axis=-1, keepdims=True)
        o = jnp.dot(e.astype(BF16), v_ref[keys, cols], preferred_element_type=F32)
        o = o * (1.0 / l)
        o_ref[i * GRID_W:(i + 1) * GRID_W, cols] = jnp.where(lo, o[:GRID_W], o[GRID_W:]).astype(o_ref.dtype)


def _natten(proj, bias, batch, seq_len, *, max_rows_per_step=16, lookahead=3):
    rows = seq_len // GRID_W
    n = proj.shape[0]
    fits = [r for r in (16, 8, 4, 2, 1) if r <= max_rows_per_step and rows % r == 0 and rows >= r + WIN_H - 1]
    assert fits, "sequence shorter than one neighbourhood window"
    rows_per_step = fits[0]
    block_rows = rows_per_step + WIN_H - 1
    steps = rows // rows_per_step

    def block_start(j):
        return jnp.clip(j * rows_per_step - WIN_H // 2, 0, rows - block_rows)

    return pl.pallas_call(
        functools.partial(_natten_kernel, rows=rows, rows_per_step=rows_per_step, lookahead=lookahead),
        grid=(batch, steps),
        in_specs=[
            pl.BlockSpec((rows_per_step * GRID_W, NA_WIDTH), lambda b, j: (b * steps + j, QA_OFF // NA_WIDTH)),
            pl.BlockSpec((pl.Element(block_rows * GRID_W), pl.Element(NA_WIDTH)),
                         lambda b, j: ((b * rows + block_start(j)) * GRID_W, KA_OFF)),
            pl.BlockSpec((pl.Element(block_rows * GRID_W), pl.Element(NA_WIDTH)),
                         lambda b, j: ((b * rows + block_start(j)) * GRID_W, VA_OFF)),
            pl.BlockSpec((WIN_H, NA_HEADS * GRID_W, WIN_H * GRID_W), lambda b, j: (0, 0, 0)),
        ],
        out_specs=pl.BlockSpec((rows_per_step * GRID_W, NA_WIDTH), lambda b, j: (b * steps + j, 0)),
        out_shape=jax.ShapeDtypeStruct((n, NA_WIDTH), BF16),
        compiler_params=_params("parallel", "arbitrary"),
        name="natten",
    )(proj, proj, proj, bias)


def _natten_bias(rpb):
    qc = np.arange(GRID_W)[:, None]
    kc = np.arange(GRID_W)[None, :]
    wstart = np.clip(qc - WIN_W // 2, 0, GRID_W - WIN_W)
    in_win = (kc >= wstart) & (kc < wstart + WIN_W)
    dc = np.clip(kc - qc + WIN_W - 1, 0, 2 * WIN_W - 2)
    dr = np.arange(WIN_H)[:, None] + np.arange(WIN_H)[None, :]
    pick_dc = (dc[..., None] == np.arange(2 * WIN_W - 1)).astype(np.float32)
    pick_dr = (dr[..., None] == np.arange(2 * WIN_H - 1)).astype(np.float32)
    b = jnp.einsum("tsr,hrd,qkd->thqsk", pick_dr, rpb.astype(F32), pick_dc,
                   precision=lax.Precision.HIGHEST)
    b = jnp.where(in_win[None, None, :, None, :], b, NEG_INF)
    return b.reshape(WIN_H, NA_HEADS * GRID_W, WIN_H * GRID_W)


def _diff_kernel(sc_ref, q_ref, k_ref, vt_ref, g_ref, o_ref,
                 qst_scr, s_a, s_b, cm_a, cm_b, p_a, p_b, al_a, al_b, m_scr, acc_scr, *, tk, heads, stabilize):
    tq = q_ref.shape[0]
    n_chunks = k_ref.shape[0] // tk
    bufs = ((s_a, cm_a, p_a, al_a), (s_b, cm_b, p_b, al_b))

    def head_cols(h):
        return slice(h * DIFF_V_DIM, (h + 1) * DIFF_V_DIM)

    def prepare(h):
        qt = q_ref[:, head_cols(h)].astype(F32).T
        row = lax.broadcasted_iota(jnp.int32, qt.shape, 0)
        zero = jnp.zeros_like(qt)
        qst_scr[h] = jnp.concatenate(
            [jnp.where(row < HEAD_DIM, qt, zero), jnp.where(row < HEAD_DIM, zero, qt)], axis=1).astype(BF16)
        m_scr[h] = jnp.full(m_scr.shape[1:], NEG_INF if stabilize else 0.0, F32)
        acc_scr[h] = jnp.zeros(acc_scr.shape[1:], F32)

    def scores(h, c, buf):
        s_ref, cm_ref, p_ref, _ = buf
        start = pl.multiple_of(c * tk, tk)
        s = jnp.dot(k_ref[pl.ds(start, tk), head_cols(h)], qst_scr[h], preferred_element_type=F32)
        if stabilize:
            s_ref[...] = s
            cm_ref[...] = jnp.max(s, axis=0, keepdims=True)
        else:
            p = jnp.exp2(s)
            p_ref[...] = p.astype(BF16)
            m_scr[h] = m_scr[h] + jnp.sum(p, axis=0, keepdims=True)

    def softmax(h, buf):
        s_ref, cm_ref, p_ref, al_ref = buf
        m_prev = m_scr[h]
        m_new = jnp.maximum(m_prev, cm_ref[...])
        al_ref[...] = jnp.exp2(m_prev - m_new)
        m_scr[h] = m_new
        p_ref[...] = jnp.exp2(s_ref[...] - m_new).astype(BF16)

    def accumulate(h, c, buf):
        _, _, p_ref, al_ref = buf
        if stabilize:
            pv = jnp.dot(vt_ref[h, c], p_ref[...], preferred_element_type=F32)
            acc_scr[h] = al_ref[...] * acc_scr[h] + pv
        else:
            pv = jnp.dot(vt_ref[h, c, :DIFF_V_DIM, :], p_ref[...], preferred_element_type=F32)
            acc_scr[h, :DIFF_V_DIM, :] = acc_scr[h, :DIFF_V_DIM, :] + pv

    def finalize(h):
        acc = acc_scr[h]
        total = acc[DIFF_V_DIM:DIFF_V_DIM + 1] if stabilize else m_scr[h]
        o = acc[:DIFF_V_DIM] * (1.0 / total)
        d = o[:, :tq] - sc_ref[0] * o[:, tq:]
        ms = jnp.mean(d * d, axis=0, keepdims=True)
        y = d * lax.rsqrt(ms + SUBLN_EPS) * (g_ref[...] * sc_ref[1])
        o_ref[:, head_cols(h)] = y.T.astype(o_ref.dtype)

    def turn(h, c, parity, nxt, prev):
        if nxt is not None:
            if nxt[0] != h:
                prepare(nxt[0])
            scores(nxt[0], nxt[1], bufs[1 - parity])
        if not stabilize:
            accumulate(h, c, bufs[parity])
            if nxt is None or nxt[0] != h:
                finalize(h)
            return
        softmax(h, bufs[parity])
        if prev is not None:
            accumulate(prev[0], prev[1], bufs[1 - parity])
            if prev[0] != h:
                finalize(prev[0])

    prepare(0)
    scores(0, 0, bufs[0])
    if heads == 1 and n_chunks > 4 and (n_chunks - 4) % 4 == 0:
        turn(0, 0, 0, (0, 1), None)
        turn(0, 1, 1, (0, 2), (0, 0))

        def body(i, carry):
            for j in range(4):
                c = 2 + 4 * i + j
                turn(0, c, j % 2, (0, c + 1), (0, c - 1))
            return carry
        lax.fori_loop(0, (n_chunks - 4) // 4, body, 0)
        turn(0, n_chunks - 2, 0, (0, n_chunks - 1), (0, n_chunks - 3))
        turn(0, n_chunks - 1, 1, None, (0, n_chunks - 2))
    else:
        tasks = [(h, c) for h in range(heads) for c in range(n_chunks)]
        for t, (h, c) in enumerate(tasks):
            turn(h, c, t % 2, tasks[t + 1] if t + 1 < len(tasks) else None, tasks[t - 1] if t else None)
    if stabilize:
        accumulate(heads - 1, n_chunks - 1, bufs[1])
        finalize(heads - 1)


def _diffattn(proj, vt, scalars, subln_col, batch, seq_len, *, stabilize, tq=512, max_unrolled_tasks=32):
    n = proj.shape[0]
    tk = vt.shape[-1]
    n_chunks = seq_len // tk
    q_tiles = seq_len // tq
    assert n_chunks % 2 == 0
    heads = DIFF_HEADS if DIFF_HEADS * n_chunks <= max_unrolled_tasks else 1
    width = heads * DIFF_V_DIM
    return pl.pallas_call(
        functools.partial(_diff_kernel, tk=tk, heads=heads, stabilize=stabilize),
        grid=(batch, DIFF_HEADS // heads, q_tiles),
        in_specs=[
            pl.BlockSpec(memory_space=pltpu.SMEM),
            pl.BlockSpec((tq, width), lambda b, h, i: (b * q_tiles + i, QB_OFF // width + h)),
            pl.BlockSpec((seq_len, width), lambda b, h, i: (b, KB_OFF // width + h)),
            pl.BlockSpec((None, heads, n_chunks, DIFF_VT_ROWS, tk), lambda b, h, i: (b, h, 0, 0, 0)),
            pl.BlockSpec((DIFF_V_DIM, 1), lambda b, h, i: (0, 0)),
        ],
        out_specs=pl.BlockSpec((tq, width), lambda b, h, i: (b * q_tiles + i, h)),
        out_shape=jax.ShapeDtypeStruct((n, DIFF_HEADS * DIFF_V_DIM), BF16),
        scratch_shapes=[
            pltpu.VMEM((heads, DIFF_V_DIM, 2 * tq), BF16),
            pltpu.VMEM((tk, 2 * tq), F32),
            pltpu.VMEM((tk, 2 * tq), F32),
            pltpu.VMEM((1, 2 * tq), F32),
            pltpu.VMEM((1, 2 * tq), F32),
            pltpu.VMEM((tk, 2 * tq), BF16),
            pltpu.VMEM((tk, 2 * tq), BF16),
            pltpu.VMEM((1, 2 * tq), F32),
            pltpu.VMEM((1, 2 * tq), F32),
            pltpu.VMEM((heads, 1, 2 * tq), F32),
            pltpu.VMEM((heads, DIFF_VT_ROWS, 2 * tq), F32),
        ],
        compiler_params=_params("parallel", "parallel", "arbitrary"),
        name="diffattn",
    )(scalars, proj, proj, vt, subln_col)


def _merge_ffn_kernel(x_ref, ya_ref, yb_ref, ga_ref, gb_ref, wa_ref, wb_ref, wo_ref,
                      g_ref, ffn_wi_ref, ffn_wo_ref, o_ref, *, chunk):
    def stages(rows):
        a = jnp.dot(ya_ref[rows, :], wa_ref[...], preferred_element_type=F32)
        b = jnp.dot(yb_ref[rows, :], wb_ref[...], preferred_element_type=F32)
        yield
        merged = ga_ref[rows, :].astype(F32) * a + gb_ref[rows, :].astype(F32) * b
        x = x_ref[rows, :] + jnp.dot(merged.astype(BF16), wo_ref[...], preferred_element_type=F32)
        yield
        o_ref[rows, :] = yield from _ffn_stages(x, g_ref, ffn_wi_ref, ffn_wo_ref, chunk)

    half = x_ref.shape[0] // 2
    _interleave(stages(slice(0, half)), stages(slice(half, 2 * half)))


def _merge_ffn(x, ya, yb, proj, wa, wb, wo, g, ffn_wi, ffn_wo, *, layer, tm=512, chunk=256):
    n = x.shape[0]
    of_layer = lambda i: (layer, 0, 0)
    return pl.pallas_call(
        functools.partial(_merge_ffn_kernel, chunk=chunk),
        grid=(n // tm,),
        in_specs=[
            pl.BlockSpec((tm, D_MODEL), lambda i: (i, 0)),
            pl.BlockSpec((tm, NA_WIDTH), lambda i: (i, 0)),
            pl.BlockSpec((tm, NA_WIDTH), lambda i: (i, 0)),
            pl.BlockSpec((tm, D_MODEL), lambda i: (i, GA_OFF // D_MODEL)),
            pl.BlockSpec((tm, D_MODEL), lambda i: (i, GB_OFF // D_MODEL)),
            pl.BlockSpec((None, NA_WIDTH, D_MODEL), of_layer),
            pl.BlockSpec((None, NA_WIDTH, D_MODEL), of_layer),
            pl.BlockSpec((None, D_MODEL, D_MODEL), of_layer),
            pl.BlockSpec((1, D_MODEL), lambda i: (0, 0)),
            pl.BlockSpec((None, D_MODEL, 2 * D_FF), of_layer),
            pl.BlockSpec((None, D_FF, D_MODEL), of_layer),
        ],
        out_specs=pl.BlockSpec((tm, D_MODEL), lambda i: (i, 0)),
        out_shape=jax.ShapeDtypeStruct((n, D_MODEL), F32),
        compiler_params=_params("parallel"),
        name="merge_ffn",
    )(x, ya, yb, proj, proj, wa, wb, wo, g, ffn_wi, ffn_wo)


def _rope_tables(seq_len):
    half = ROPE_DIM // 2
    inv_freq = jnp.power(ROPE_THETA, -jnp.arange(half, dtype=F32) * 2.0 / ROPE_DIM)
    ang = jnp.arange(seq_len, dtype=F32)[:, None] * inv_freq[None, :]
    cos, sin = jnp.cos(ang), jnp.sin(ang)
    ones = jnp.ones((seq_len, HEAD_DIM - ROPE_DIM), F32)
    zeros = jnp.zeros((seq_len, HEAD_DIM - ROPE_DIM), F32)
    zh = jnp.zeros_like(sin)
    c = jnp.concatenate([cos, cos, ones], axis=1)
    sa = jnp.concatenate([-sin, zh, zeros], axis=1)
    sb = jnp.concatenate([zh, sin, zeros], axis=1)
    reps = V7X_LANES // HEAD_DIM
    return tuple(jnp.tile(t, (1, reps)) for t in (c, sa, sb))


def _layer_consts(l, w):
    scale = HEAD_DIM ** -0.5
    heads = PROJ_BLOCK // HEAD_DIM
    head_gains = jnp.stack([
        jnp.tile(w["qa_norm"][l] * (scale * LOG2_E), heads), jnp.tile(w["ka_norm"][l], heads),
        jnp.tile(w["qb_norm"][l] * (scale * LOG2_E), heads), jnp.tile(w["kb_norm"][l], heads)])
    blk = np.arange(PROJ_BLOCK) // HEAD_DIM
    blockdiag = jnp.asarray((blk[:, None] == blk[None, :]).astype(np.float32) / HEAD_DIM, BF16)
    lam_init = 0.8 - 0.6 * math.exp(-0.3 * l)
    lam = (jnp.exp(jnp.sum(w["lam_q1"][l] * w["lam_k1"][l]))
           - jnp.exp(jnp.sum(w["lam_q2"][l] * w["lam_k2"][l])) + lam_init)
    score_bound = (HEAD_DIM * NORM_SLACK * jnp.max(jnp.abs(w["qb_norm"][l])) * (scale * LOG2_E)
                   * jnp.max(jnp.abs(w["kb_norm"][l])))
    return dict(
        layer=l, score_bound=score_bound,
        ffn1_g=w["ffn1_norm"][l][None], mix_g=w["mix_norm"][l][None], ffn2_g=w["ffn2_norm"][l][None],
        head_gains=head_gains, blockdiag=blockdiag, bias=_natten_bias(w["rpb"][l] * LOG2_E),
        scalars=jnp.stack([lam, jnp.asarray(1.0 - lam_init, F32)]).astype(F32),
        subln=w["subln"][l][:, None],
    )


def _encoder_layer(x, c, mats, rope, batch, seq_len):
    x, proj, vt = _ffn_proj(x, c["ffn1_g"], mats["ffn1_wi"], mats["ffn1_wo"], c["mix_g"], mats["w_in"],
                            c["head_gains"], c["blockdiag"], *rope, batch, seq_len,
                            layer=c["layer"], tk=_diff_key_chunk(seq_len))
    ya = _natten(proj, c["bias"], batch, seq_len)
    yb = lax.cond(
        c["score_bound"] <= DIFF_MAX_UNSTABILIZED_SCORE,
        lambda: _diffattn(proj, vt, c["scalars"], c["subln"], batch, seq_len, stabilize=False),
        lambda: _diffattn(proj, vt, c["scalars"], c["subln"], batch, seq_len, stabilize=True))
    return _merge_ffn(x, ya, yb, proj, mats["w_a_out"], mats["w_b_out"], mats["w_o"], c["ffn2_g"],
                      mats["ffn2_wi"], mats["ffn2_wo"], layer=c["layer"])


MATMUL_WEIGHTS = ("ffn1_wi", "ffn1_wo", "w_in", "w_a_out", "w_b_out", "w_o", "ffn2_wi", "ffn2_wo")


def _encoder(groups, weights, depth):
    mats = {name: weights[name].astype(BF16) for name in MATMUL_WEIGHTS}
    consts = [_layer_consts(l, weights) for l in range(depth)]
    outs = []
    for x in groups:
        batch, seq_len, _ = x.shape
        rope = _rope_tables(seq_len)
        y = x.reshape(batch * seq_len, D_MODEL)
        for c in consts:
            y = _encoder_layer(y, c, mats, rope, batch, seq_len)
        outs.append(y.reshape(x.shape))
    return tuple(outs)


def kernel(x_prompt, x_sample, ffn1_norm, ffn1_wi, ffn1_wo, mix_norm, w_in, qa_norm, ka_norm, rpb, qb_norm, kb_norm, lam_q1, lam_k1, lam_q2, lam_k2, subln, w_a_out, w_b_out, w_o, ffn2_norm, ffn2_wi, ffn2_wo):
    weights = dict(ffn1_norm=ffn1_norm, ffn1_wi=ffn1_wi, ffn1_wo=ffn1_wo, mix_norm=mix_norm, w_in=w_in,
                   qa_norm=qa_norm, ka_norm=ka_norm, rpb=rpb, qb_norm=qb_norm, kb_norm=kb_norm,
                   lam_q1=lam_q1, lam_k1=lam_k1, lam_q2=lam_q2, lam_k2=lam_k2, subln=subln,
                   w_a_out=w_a_out, w_b_out=w_b_out, w_o=w_o,
                   ffn2_norm=ffn2_norm, ffn2_wi=ffn2_wi, ffn2_wo=ffn2_wo)
    return _encoder((x_prompt, x_sample), weights, ffn1_norm.shape[0])
```

```python
import functools
import math

import jax
import jax.numpy as jnp
import numpy as np
from jax import lax
from jax.experimental import pallas as pl
from jax.experimental.pallas import tpu as pltpu

F32 = jnp.float32
BF16 = jnp.bfloat16

D_MODEL = 1024
GRID_W = 64
HEAD_DIM = 64
NA_HEADS = 8
DIFF_HEADS = 4
NA_WIDTH = NA_HEADS * HEAD_DIM
DIFF_V_DIM = 2 * HEAD_DIM
IN_WIDTH = 5120
D_FF = 2816
WIN_H = 8
WIN_W = 16
ROPE_THETA = 500000.0
ROPE_DIM = HEAD_DIM // 4
NORM_EPS = 1e-6
SUBLN_EPS = 1e-5
NEG_INF = -1e30
FFN_RES = 0.5

V7X_VMEM_BYTES = 64 * 1024 * 1024
V7X_LANES = 128
VMEM_LIMIT = V7X_VMEM_BYTES * 7 // 8

PROJ_BLOCK = 512
W_QA, W_KA, W_VA, W_QB, W_KB, W_VB, W_GA, W_GB = 0, 512, 1024, 1536, 2048, 2560, 3072, 4096
GA_OFF, GB_OFF, QA_OFF, KA_OFF, VA_OFF, QB_OFF, KB_OFF = 0, 1024, 2048, 2560, 3072, 3584, 4096
PROJ_OUT = 4608
DIFF_TK_CHOICES = (1024, 512, 256)
DIFF_MIN_CHUNKS = 8
BF16_SUBLANES = 16
DIFF_VT_ROWS = DIFF_V_DIM + BF16_SUBLANES
LOG2_E = math.log2(math.e)
NORM_SLACK = 1.05
DIFF_MAX_UNSTABILIZED_SCORE = 32.0


def _params(*semantics):
    return pltpu.CompilerParams(dimension_semantics=semantics, vmem_limit_bytes=VMEM_LIMIT)


def _rms(x, g, eps):
    return x * lax.rsqrt(jnp.mean(x * x, axis=-1, keepdims=True) + eps) * g


def _interleave(*stage_generators):
    live = list(stage_generators)
    while live:
        for gen in list(live):
            try:
                next(gen)
            except StopIteration:
                live.remove(gen)


def _ffn_stages(x, g_ref, wi_ref, wo_ref, chunk):
    h = _rms(x, g_ref[...], NORM_EPS).astype(BF16)
    n_chunks = D_FF // chunk

    def up(c):
        a = jnp.dot(h, wi_ref[:, c * chunk:(c + 1) * chunk], preferred_element_type=F32)
        b = jnp.dot(h, wi_ref[:, D_FF + c * chunk:D_FF + (c + 1) * chunk], preferred_element_type=F32)
        return a, b

    acc = jnp.zeros(x.shape, F32)
    nxt = up(0)
    yield
    for c in range(n_chunks):
        a, b = nxt
        if c + 1 < n_chunks:
            nxt = up(c + 1)
        gate = (a * jax.nn.sigmoid(a) * b).astype(BF16)
        acc = acc + jnp.dot(gate, wo_ref[c * chunk:(c + 1) * chunk, :], preferred_element_type=F32)
        yield
    return x + FFN_RES * acc


def _ffn_proj_kernel(x_ref, ffn_g_ref, ffn_wi_ref, ffn_wo_ref, g_ref, w_ref, hg_ref, bd_ref,
                     cos_ref, sa_ref, sb_ref, x_out_ref, o_ref, vt_ref, *, chunk):
    tile_rows = x_ref.shape[0]
    n_slabs, slab = vt_ref.shape[1], vt_ref.shape[-1]

    def stages(r0, nrows):
        rows = slice(r0, r0 + nrows)
        x = yield from _ffn_stages(x_ref[rows, :], ffn_g_ref, ffn_wi_ref, ffn_wo_ref, chunk)
        x_out_ref[rows, :] = x
        h = _rms(x, g_ref[...], NORM_EPS).astype(BF16)

        def proj(w_off):
            return jnp.dot(h, w_ref[:, w_off:w_off + PROJ_BLOCK], preferred_element_type=F32)

        def head_norm(p, gain_row):
            ms = jnp.dot((p * p).astype(BF16), bd_ref[...], preferred_element_type=F32)
            return p * lax.rsqrt(ms + NORM_EPS) * hg_ref[gain_row:gain_row + 1, :]

        reps = PROJ_BLOCK // V7X_LANES
        cos = jnp.concatenate([cos_ref[rows, :]] * reps, axis=1)
        sa = jnp.concatenate([sa_ref[rows, :]] * reps, axis=1)
        sb = jnp.concatenate([sb_ref[rows, :]] * reps, axis=1)

        def rope(y):
            half = ROPE_DIM // 2
            up = pltpu.roll(y, PROJ_BLOCK - half, 1)
            dn = pltpu.roll(y, half, 1)
            return y * cos + up * sa + dn * sb

        def put(off, val):
            o_ref[rows, off:off + PROJ_BLOCK] = val.astype(o_ref.dtype)

        def put_diff_values(vb):
            width = min(nrows, slab)
            pad_row = lax.broadcasted_iota(jnp.int32, (BF16_SUBLANES, width), 0)
            ones_rows = jnp.where(pad_row == 0, 1.0, 0.0).astype(vt_ref.dtype)
            for head in range(DIFF_HEADS):
                for start in range(0, nrows, width):
                    c, lane0 = (r0 + start) // slab, (r0 + start) % slab
                    blk = vb[start:start + width, head * DIFF_V_DIM:(head + 1) * DIFF_V_DIM]
                    vt_ref[head, c, :DIFF_V_DIM, lane0:lane0 + width] = blk.T.astype(vt_ref.dtype)
                    vt_ref[head, c, DIFF_V_DIM:, lane0:lane0 + width] = ones_rows

        blocks = [
            (W_QA, lambda p: put(QA_OFF, head_norm(p, 0))),
            (W_GA, lambda p: put(GA_OFF, jax.nn.sigmoid(p))),
            (W_KA, lambda p: put(KA_OFF, head_norm(p, 1))),
            (W_GA + PROJ_BLOCK, lambda p: put(GA_OFF + PROJ_BLOCK, jax.nn.sigmoid(p))),
            (W_QB, lambda p: put(QB_OFF, rope(head_norm(p, 2)))),
            (W_GB, lambda p: put(GB_OFF, jax.nn.sigmoid(p))),
            (W_KB, lambda p: put(KB_OFF, rope(head_norm(p, 3)))),
            (W_GB + PROJ_BLOCK, lambda p: put(GB_OFF + PROJ_BLOCK, jax.nn.sigmoid(p))),
            (W_VB, put_diff_values),
            (W_VA, lambda p: put(VA_OFF, p)),
        ]
        nxt = proj(blocks[0][0])
        yield
        for n, (_, epilogue) in enumerate(blocks):
            p = nxt
            if n + 1 < len(blocks):
                nxt = proj(blocks[n + 1][0])
            epilogue(p)
            yield

    assert n_slabs * slab == tile_rows
    half = tile_rows // 2
    _interleave(stages(0, half), stages(half, half))


def _diff_key_chunk(seq_len):
    fits = [t for t in DIFF_TK_CHOICES if seq_len % (2 * t) == 0]
    deep = [t for t in fits if seq_len // t >= DIFF_MIN_CHUNKS]
    return (deep or fits[-1:])[0]


def _ffn_proj(x, ffn_g, ffn_wi, ffn_wo, g, w, head_gains, blockdiag, cos, sa, sb, batch, seq_len, *,
              layer, tk, tm=512, chunk=256):
    n = x.shape[0]
    tiles_per_seq = seq_len // tm
    if tm >= tk:
        vt_spec = pl.BlockSpec((None, DIFF_HEADS, tm // tk, DIFF_VT_ROWS, tk),
                               lambda i: (i // tiles_per_seq, 0, i % tiles_per_seq, 0, 0))
    else:
        sub = tk // tm
        vt_spec = pl.BlockSpec((None, DIFF_HEADS, 1, DIFF_VT_ROWS, tm),
                               lambda i: (i // tiles_per_seq, 0, (i % tiles_per_seq) // sub, 0,
                                          (i % tiles_per_seq) % sub))
    const = lambda i: (0, 0)
    of_layer = lambda i: (layer, 0, 0)
    return pl.pallas_call(
        functools.partial(_ffn_proj_kernel, chunk=chunk),
        grid=(n // tm,),
        in_specs=[
            pl.BlockSpec((tm, D_MODEL), lambda i: (i, 0)),
            pl.BlockSpec((1, D_MODEL), const),
            pl.BlockSpec((None, D_MODEL, 2 * D_FF), of_layer),
            pl.BlockSpec((None, D_FF, D_MODEL), of_layer),
            pl.BlockSpec((1, D_MODEL), const),
            pl.BlockSpec((None, D_MODEL, IN_WIDTH), of_layer),
            pl.BlockSpec((4, PROJ_BLOCK), const),
            pl.BlockSpec((PROJ_BLOCK, PROJ_BLOCK), const),
            pl.BlockSpec((tm, V7X_LANES), lambda i: (i % tiles_per_seq, 0)),
            pl.BlockSpec((tm, V7X_LANES), lambda i: (i % tiles_per_seq, 0)),
            pl.BlockSpec((tm, V7X_LANES), lambda i: (i % tiles_per_seq, 0)),
        ],
        out_specs=[
            pl.BlockSpec((tm, D_MODEL), lambda i: (i, 0)),
            pl.BlockSpec((tm, PROJ_OUT), lambda i: (i, 0)),
            vt_spec,
        ],
        out_shape=[
            jax.ShapeDtypeStruct((n, D_MODEL), F32),
            jax.ShapeDtypeStruct((n, PROJ_OUT), BF16),
            jax.ShapeDtypeStruct((batch, DIFF_HEADS, seq_len // tk, DIFF_VT_ROWS, tk), BF16),
        ],
        compiler_params=_params("parallel"),
        name="ffn_proj",
    )(x, ffn_g, ffn_wi, ffn_wo, g, w, head_gains, blockdiag, cos, sa, sb)


def _natten_kernel(q_ref, k_ref, v_ref, bias_ref, o_ref, *, rows, rows_per_step, lookahead):
    n_keys = WIN_H * GRID_W
    r0 = pl.program_id(1) * rows_per_step
    block_start = jnp.clip(r0 - WIN_H // 2, 0, rows - (rows_per_step + WIN_H - 1))
    lane = lax.broadcasted_iota(jnp.int32, (GRID_W, V7X_LANES), 1)
    lo = lane < HEAD_DIM
    ones_cols = jnp.ones((n_keys, V7X_LANES), BF16)
    tasks = [(i, pair) for i in range(rows_per_step) for pair in range(NA_HEADS // 2)]

    def key_rows(i):
        start = jnp.clip(r0 + i - WIN_H // 2, 0, rows - WIN_H)
        variant = start - (r0 + i) + WIN_H - 1
        return pl.ds(pl.multiple_of((start - block_start) * GRID_W, GRID_W), n_keys), variant

    def scores(i, pair):
        keys, variant = key_rows(i)
        cols = slice(pair * V7X_LANES, (pair + 1) * V7X_LANES)
        qp = q_ref[i * GRID_W:(i + 1) * GRID_W, cols]
        zero = jnp.zeros_like(qp)
        lhs = jnp.concatenate([jnp.where(lo, qp, zero), jnp.where(lo, zero, qp)], axis=0)
        s = lax.dot_general(lhs, k_ref[keys, cols], (((1,), (1,)), ((), ())), preferred_element_type=F32)
        return s + bias_ref[variant, pair * 2 * GRID_W:(pair + 1) * 2 * GRID_W, :]

    pending = [scores(*task) for task in tasks[:lookahead]]
    for n, (i, pair) in enumerate(tasks):
        s = pending.pop(0)
        if n + lookahead < len(tasks):
            pending.append(scores(*tasks[n + lookahead]))
        keys, _ = key_rows(i)
        cols = slice(pair * V7X_LANES, (pair + 1) * V7X_LANES)
        m = jnp.max(s, axis=-1, keepdims=True)
        e = jnp.exp2(s - m)
        v_ext = jnp.concatenate([v_ref[keys, cols], ones_cols], axis=1)
        o = jnp.dot(e.astype(BF16), v_ext, preferred_element_type=F32)
        o = o[:, :V7X_LANES] * (1.0 / o[:, V7X_LANES:])
        o_ref[i * GRID_W:(i + 1) * GRID_W, cols] = jnp.where(lo, o[:GRID_W], o[GRID_W:]).astype(o_ref.dtype)


def _natten(proj, bias, batch, seq_len, *, max_rows_per_step=16, lookahead=3):
    rows = seq_len // GRID_W
    n = proj.shape[0]
    fits = [r for r in (16, 8, 4, 2, 1) if r <= max_rows_per_step and rows % r == 0 and rows >= r + WIN_H - 1]
    assert fits, "sequence shorter than one neighbourhood window"
    rows_per_step = fits[0]
    block_rows = rows_per_step + WIN_H - 1
    steps = rows // rows_per_step

    def block_start(j):
        return jnp.clip(j * rows_per_step - WIN_H // 2, 0, rows - block_rows)

    return pl.pallas_call(
        functools.partial(_natten_kernel, rows=rows, rows_per_step=rows_per_step, lookahead=lookahead),
        grid=(batch, steps),
        in_specs=[
            pl.BlockSpec((rows_per_step * GRID_W, NA_WIDTH), lambda b, j: (b * steps + j, QA_OFF // NA_WIDTH)),
            pl.BlockSpec((pl.Element(block_rows * GRID_W), pl.Element(NA_WIDTH)),
                         lambda b, j: ((b * rows + block_start(j)) * GRID_W, KA_OFF)),
            pl.BlockSpec((pl.Element(block_rows * GRID_W), pl.Element(NA_WIDTH)),
                         lambda b, j: ((b * rows + block_start(j)) * GRID_W, VA_OFF)),
            pl.BlockSpec((WIN_H, NA_HEADS * GRID_W, WIN_H * GRID_W), lambda b, j: (0, 0, 0)),
        ],
        out_specs=pl.BlockSpec((rows_per_step * GRID_W, NA_WIDTH), lambda b, j: (b * steps + j, 0)),
        out_shape=jax.ShapeDtypeStruct((n, NA_WIDTH), BF16),
        compiler_params=_params("parallel", "arbitrary"),
        name="natten",
    )(proj, proj, proj, bias)


def _natten_bias(rpb):
    qc = np.arange(GRID_W)[:, None]
    kc = np.arange(GRID_W)[None, :]
    wstart = np.clip(qc - WIN_W // 2, 0, GRID_W - WIN_W)
    in_win = (kc >= wstart) & (kc < wstart + WIN_W)
    dc = np.clip(kc - qc + WIN_W - 1, 0, 2 * WIN_W - 2)
    dr = np.arange(WIN_H)[:, None] + np.arange(WIN_H)[None, :]
    pick_dc = (dc[..., None] == np.arange(2 * WIN_W - 1)).astype(np.float32)
    pick_dr = (dr[..., None] == np.arange(2 * WIN_H - 1)).astype(np.float32)
    b = jnp.einsum("tsr,hrd,qkd->thqsk", pick_dr, rpb.astype(F32), pick_dc,
                   precision=lax.Precision.HIGHEST)
    b = jnp.where(in_win[None, None, :, None, :], b, NEG_INF)
    return b.reshape(WIN_H, NA_HEADS * GRID_W, WIN_H * GRID_W)


def _diff_kernel(sc_ref, q_ref, k_ref, vt_ref, g_ref, o_ref,
                 qst_scr, s_a, s_b, cm_a, cm_b, p_a, p_b, al_a, al_b, m_scr, acc_scr, *, tk, heads, stabilize):
    tq = q_ref.shape[0]
    n_chunks = k_ref.shape[0] // tk
    bufs = ((s_a, cm_a, p_a, al_a), (s_b, cm_b, p_b, al_b))

    def head_cols(h):
        return slice(h * DIFF_V_DIM, (h + 1) * DIFF_V_DIM)

    def prepare(h):
        qt = q_ref[:, head_cols(h)].astype(F32).T
        row = lax.broadcasted_iota(jnp.int32, qt.shape, 0)
        zero = jnp.zeros_like(qt)
        qst_scr[h] = jnp.concatenate(
            [jnp.where(row < HEAD_DIM, qt, zero), jnp.where(row < HEAD_DIM, zero, qt)], axis=1).astype(BF16)
        m_scr[h] = jnp.full(m_scr.shape[1:], NEG_INF if stabilize else 0.0, F32)
        acc_scr[h] = jnp.zeros(acc_scr.shape[1:], F32)

    def scores(h, c, buf):
        s_ref, cm_ref, p_ref, _ = buf
        start = pl.multiple_of(c * tk, tk)
        s = jnp.dot(k_ref[pl.ds(start, tk), head_cols(h)], qst_scr[h], preferred_element_type=F32)
        if stabilize:
            s_ref[...] = s
            cm_ref[...] = jnp.max(s, axis=0, keepdims=True)
        else:
            p = jnp.exp2(s)
            p_ref[...] = p.astype(BF16)
            m_scr[h] = m_scr[h] + jnp.sum(p, axis=0, keepdims=True)

    def softmax(h, buf):
        s_ref, cm_ref, p_ref, al_ref = buf
        m_prev = m_scr[h]
        m_new = jnp.maximum(m_prev, cm_ref[...])
        al_ref[...] = jnp.exp2(m_prev - m_new)
        m_scr[h] = m_new
        p_ref[...] = jnp.exp2(s_ref[...] - m_new).astype(BF16)

    def accumulate(h, c, buf):
        _, _, p_ref, al_ref = buf
        if stabilize:
            pv = jnp.dot(vt_ref[h, c], p_ref[...], preferred_element_type=F32)
            acc_scr[h] = al_ref[...] * acc_scr[h] + pv
        else:
            pv = jnp.dot(vt_ref[h, c, :DIFF_V_DIM, :], p_ref[...], preferred_element_type=F32)
            acc_scr[h, :DIFF_V_DIM, :] = acc_scr[h, :DIFF_V_DIM, :] + pv

    def finalize(h):
        acc = acc_scr[h]
        total = acc[DIFF_V_DIM:DIFF_V_DIM + 1] if stabilize else m_scr[h]
        o = acc[:DIFF_V_DIM] * (1.0 / total)
        d = o[:, :tq] - sc_ref[0] * o[:, tq:]
        ms = jnp.mean(d * d, axis=0, keepdims=True)
        y = d * lax.rsqrt(ms + SUBLN_EPS) * (g_ref[...] * sc_ref[1])
        o_ref[:, head_cols(h)] = y.T.astype(o_ref.dtype)

    def turn(h, c, parity, nxt, prev):
        if nxt is not None:
            if nxt[0] != h:
                prepare(nxt[0])
            scores(nxt[0], nxt[1], bufs[1 - parity])
        if not stabilize:
            accumulate(h, c, bufs[parity])
            if nxt is None or nxt[0] != h:
                finalize(h)
            return
        softmax(h, bufs[parity])
        if prev is not None:
            accumulate(prev[0], prev[1], bufs[1 - parity])
            if prev[0] != h:
                finalize(prev[0])

    prepare(0)
    scores(0, 0, bufs[0])
    if heads == 1 and n_chunks > 4 and (n_chunks - 4) % 4 == 0:
        turn(0, 0, 0, (0, 1), None)
        turn(0, 1, 1, (0, 2), (0, 0))

        def body(i, carry):
            for j in range(4):
                c = 2 + 4 * i + j
                turn(0, c, j % 2, (0, c + 1), (0, c - 1))
            return carry
        lax.fori_loop(0, (n_chunks - 4) // 4, body, 0)
        turn(0, n_chunks - 2, 0, (0, n_chunks - 1), (0, n_chunks - 3))
        turn(0, n_chunks - 1, 1, None, (0, n_chunks - 2))
    else:
        tasks = [(h, c) for h in range(heads) for c in range(n_chunks)]
        for t, (h, c) in enumerate(tasks):
            turn(h, c, t % 2, tasks[t + 1] if t + 1 < len(tasks) else None, tasks[t - 1] if t else None)
    if stabilize:
        accumulate(heads - 1, n_chunks - 1, bufs[1])
        finalize(heads - 1)


def _diffattn(proj, vt, scalars, subln_col, batch, seq_len, *, stabilize, tq=512, max_unrolled_tasks=32):
    n = proj.shape[0]
    tk = vt.shape[-1]
    n_chunks = seq_len // tk
    q_tiles = seq_len // tq
    assert n_chunks % 2 == 0
    heads = DIFF_HEADS if DIFF_HEADS * n_chunks <= max_unrolled_tasks else 1
    width = heads * DIFF_V_DIM
    return pl.pallas_call(
        functools.partial(_diff_kernel, tk=tk, heads=heads, stabilize=stabilize),
        grid=(batch, DIFF_HEADS // heads, q_tiles),
        in_specs=[
            pl.BlockSpec(memory_space=pltpu.SMEM),
            pl.BlockSpec((tq, width), lambda b, h, i: (b * q_tiles + i, QB_OFF // width + h)),
            pl.BlockSpec((seq_len, width), lambda b, h, i: (b, KB_OFF // width + h)),
            pl.BlockSpec((None, heads, n_chunks, DIFF_VT_ROWS, tk), lambda b, h, i: (b, h, 0, 0, 0)),
            pl.BlockSpec((DIFF_V_DIM, 1), lambda b, h, i: (0, 0)),
        ],
        out_specs=pl.BlockSpec((tq, width), lambda b, h, i: (b * q_tiles + i, h)),
        out_shape=jax.ShapeDtypeStruct((n, DIFF_HEADS * DIFF_V_DIM), BF16),
        scratch_shapes=[
            pltpu.VMEM((heads, DIFF_V_DIM, 2 * tq), BF16),
            pltpu.VMEM((tk, 2 * tq), F32),
            pltpu.VMEM((tk, 2 * tq), F32),
            pltpu.VMEM((1, 2 * tq), F32),
            pltpu.VMEM((1, 2 * tq), F32),
            pltpu.VMEM((tk, 2 * tq), BF16),
            pltpu.VMEM((tk, 2 * tq), BF16),
            pltpu.VMEM((1, 2 * tq), F32),
            pltpu.VMEM((1, 2 * tq), F32),
            pltpu.VMEM((heads, 1, 2 * tq), F32),
            pltpu.VMEM((heads, DIFF_VT_ROWS, 2 * tq), F32),
        ],
        compiler_params=_params("parallel", "parallel", "arbitrary"),
        name="diffattn",
    )(scalars, proj, proj, vt, subln_col)


def _merge_ffn_kernel(x_ref, ya_ref, yb_ref, ga_ref, gb_ref, wa_ref, wb_ref, wo_ref,
                      g_ref, ffn_wi_ref, ffn_wo_ref, o_ref, *, chunk):
    def stages(rows):
        a = jnp.dot(ya_ref[rows, :], wa_ref[...], preferred_element_type=F32)
        b = jnp.dot(yb_ref[rows, :], wb_ref[...], preferred_element_type=F32)
        yield
        merged = ga_ref[rows, :].astype(F32) * a + gb_ref[rows, :].astype(F32) * b
        x = x_ref[rows, :] + jnp.dot(merged.astype(BF16), wo_ref[...], preferred_element_type=F32)
        yield
        o_ref[rows, :] = yield from _ffn_stages(x, g_ref, ffn_wi_ref, ffn_wo_ref, chunk)

    half = x_ref.shape[0] // 2
    _interleave(stages(slice(0, half)), stages(slice(half, 2 * half)))


def _merge_ffn(x, ya, yb, proj, wa, wb, wo, g, ffn_wi, ffn_wo, *, layer, tm=512, chunk=256):
    n = x.shape[0]
    of_layer = lambda i: (layer, 0, 0)
    return pl.pallas_call(
        functools.partial(_merge_ffn_kernel, chunk=chunk),
        grid=(n // tm,),
        in_specs=[
            pl.BlockSpec((tm, D_MODEL), lambda i: (i, 0)),
            pl.BlockSpec((tm, NA_WIDTH), lambda i: (i, 0)),
            pl.BlockSpec((tm, NA_WIDTH), lambda i: (i, 0)),
            pl.BlockSpec((tm, D_MODEL), lambda i: (i, GA_OFF // D_MODEL)),
            pl.BlockSpec((tm, D_MODEL), lambda i: (i, GB_OFF // D_MODEL)),
            pl.BlockSpec((None, NA_WIDTH, D_MODEL), of_layer),
            pl.BlockSpec((None, NA_WIDTH, D_MODEL), of_layer),
            pl.BlockSpec((None, D_MODEL, D_MODEL), of_layer),
            pl.BlockSpec((1, D_MODEL), lambda i: (0, 0)),
            pl.BlockSpec((None, D_MODEL, 2 * D_FF), of_layer),
            pl.BlockSpec((None, D_FF, D_MODEL), of_layer),
        ],
        out_specs=pl.BlockSpec((tm, D_MODEL), lambda i: (i, 0)),
        out_shape=jax.ShapeDtypeStruct((n, D_MODEL), F32),
        compiler_params=_params("parallel"),
        name="merge_ffn",
    )(x, ya, yb, proj, proj, wa, wb, wo, g, ffn_wi, ffn_wo)


def _rope_tables(seq_len):
    half = ROPE_DIM // 2
    inv_freq = jnp.power(ROPE_THETA, -jnp.arange(half, dtype=F32) * 2.0 / ROPE_DIM)
    ang = jnp.arange(seq_len, dtype=F32)[:, None] * inv_freq[None, :]
    cos, sin = jnp.cos(ang), jnp.sin(ang)
    ones = jnp.ones((seq_len, HEAD_DIM - ROPE_DIM), F32)
    zeros = jnp.zeros((seq_len, HEAD_DIM - ROPE_DIM), F32)
    zh = jnp.zeros_like(sin)
    c = jnp.concatenate([cos, cos, ones], axis=1)
    sa = jnp.concatenate([-sin, zh, zeros], axis=1)
    sb = jnp.concatenate([zh, sin, zeros], axis=1)
    reps = V7X_LANES // HEAD_DIM
    return tuple(jnp.tile(t, (1, reps)) for t in (c, sa, sb))


def _layer_consts(l, w):
    scale = HEAD_DIM ** -0.5
    heads = PROJ_BLOCK // HEAD_DIM
    head_gains = jnp.stack([
        jnp.tile(w["qa_norm"][l] * (scale * LOG2_E), heads), jnp.tile(w["ka_norm"][l], heads),
        jnp.tile(w["qb_norm"][l] * (scale * LOG2_E), heads), jnp.tile(w["kb_norm"][l], heads)])
    blk = np.arange(PROJ_BLOCK) // HEAD_DIM
    blockdiag = jnp.asarray((blk[:, None] == blk[None, :]).astype(np.float32) / HEAD_DIM, BF16)
    lam_init = 0.8 - 0.6 * math.exp(-0.3 * l)
    lam = (jnp.exp(jnp.sum(w["lam_q1"][l] * w["lam_k1"][l]))
           - jnp.exp(jnp.sum(w["lam_q2"][l] * w["lam_k2"][l])) + lam_init)
    score_bound = (HEAD_DIM * NORM_SLACK * jnp.max(jnp.abs(w["qb_norm"][l])) * (scale * LOG2_E)
                   * jnp.max(jnp.abs(w["kb_norm"][l])))
    return dict(
        layer=l, score_bound=score_bound,
        ffn1_g=w["ffn1_norm"][l][None], mix_g=w["mix_norm"][l][None], ffn2_g=w["ffn2_norm"][l][None],
        head_gains=head_gains, blockdiag=blockdiag, bias=_natten_bias(w["rpb"][l] * LOG2_E),
        scalars=jnp.stack([lam, jnp.asarray(1.0 - lam_init, F32)]).astype(F32),
        subln=w["subln"][l][:, None],
    )


def _encoder_layer(x, c, mats, rope, batch, seq_len):
    x, proj, vt = _ffn_proj(x, c["ffn1_g"], mats["ffn1_wi"], mats["ffn1_wo"], c["mix_g"], mats["w_in"],
                            c["head_gains"], c["blockdiag"], *rope, batch, seq_len,
                            layer=c["layer"], tk=_diff_key_chunk(seq_len))
    ya = _natten(proj, c["bias"], batch, seq_len)
    yb = lax.cond(
        c["score_bound"] <= DIFF_MAX_UNSTABILIZED_SCORE,
        lambda: _diffattn(proj, vt, c["scalars"], c["subln"], batch, seq_len, stabilize=False),
        lambda: _diffattn(proj, vt, c["scalars"], c["subln"], batch, seq_len, stabilize=True))
    return _merge_ffn(x, ya, yb, proj, mats["w_a_out"], mats["w_b_out"], mats["w_o"], c["ffn2_g"],
                      mats["ffn2_wi"], mats["ffn2_wo"], layer=c["layer"])


MATMUL_WEIGHTS = ("ffn1_wi", "ffn1_wo", "w_in", "w_a_out", "w_b_out", "w_o", "ffn2_wi", "ffn2_wo")


def _encoder(groups, weights, depth):
    mats = {name: weights[name].astype(BF16) for name in MATMUL_WEIGHTS}
    consts = [_layer_consts(l, weights) for l in range(depth)]
    outs = []
    for x in groups:
        batch, seq_len, _ = x.shape
        rope = _rope_tables(seq_len)
        y = x.reshape(batch * seq_len, D_MODEL)
        for c in consts:
            y = _encoder_layer(y, c, mats, rope, batch, seq_len)
        outs.append(y.reshape(x.shape))
    return tuple(outs)


def kernel(x_prompt, x_sample, ffn1_norm, ffn1_wi, ffn1_wo, mix_norm, w_in, qa_norm, ka_norm, rpb, qb_norm, kb_norm, lam_q1, lam_k1, lam_q2, lam_k2, subln, w_a_out, w_b_out, w_o, ffn2_norm, ffn2_wi, ffn2_wo):
    weights = dict(ffn1_norm=ffn1_norm, ffn1_wi=ffn1_wi, ffn1_wo=ffn1_wo, mix_norm=mix_norm, w_in=w_in,
                   qa_norm=qa_norm, ka_norm=ka_norm, rpb=rpb, qb_norm=qb_norm, kb_norm=kb_norm,
                   lam_q1=lam_q1, lam_k1=lam_k1, lam_q2=lam_q2, lam_k2=lam_k2, subln=subln,
                   w_a_out=w_a_out, w_b_out=w_b_out, w_o=w_o,
                   ffn2_norm=ffn2_norm, ffn2_wi=ffn2_wi, ffn2_wo=ffn2_wo)
    return _encoder((x_prompt, x_sample), weights, ffn1_norm.shape[0])
```
